```python
import math
import jax, jax.numpy as jnp
from jax import lax
import numpy as np

D_MODEL = 2048
BATCH = 16
SEQ = 2048
DEPTH = 4

N_MIXERS = 3
EPS = 1e-6
GN_EPS = 1e-5

CONV_WIDTH = D_MODEL
CONV_K = 3

RET_HEADS = 8
RET_QK_DIM = D_MODEL // RET_HEADS
RET_V_DIM = 2 * D_MODEL // RET_HEADS
RET_QK_WIDTH = RET_HEADS * RET_QK_DIM
RET_WIDTH = RET_HEADS * RET_V_DIM
RET_CHUNK = 128
ROPE_BASE = 10000.0

SB_HEADS = 16
SB_HEAD_DIM = D_MODEL // SB_HEADS
SB_WIDTH = SB_HEADS * SB_HEAD_DIM
SB_BLOCK = 128

N_CONV = (DEPTH + 2) // 3
N_RET = (DEPTH + 1) // 3
N_SB = DEPTH // 3

kernel_name = "hybrid_conv_retention_stickbreak"


def rmsnorm(x, g):
    xf = x.astype(jnp.float32)
    y = xf * lax.rsqrt(jnp.mean(xf * xf, axis=-1, keepdims=True) + EPS) * g.astype(jnp.float32)
    return y.astype(x.dtype)


def rms_f32(x, g):
    xf = x.astype(jnp.float32)
    return xf * lax.rsqrt(jnp.mean(xf * xf, axis=-1, keepdims=True) + EPS) * g.astype(jnp.float32)


def split_heads(t, n_heads, head_dim):
    b, s, _ = t.shape
    return t.reshape(b, s, n_heads, head_dim).transpose(0, 2, 1, 3)


def rotary(x, pos):
    half = x.shape[-1] // 2
    inv_freq = ROPE_BASE ** (-jnp.arange(half, dtype=jnp.float32) / half)
    ang = pos.astype(jnp.float32)[:, None] * inv_freq[None, :]
    cos, sin = jnp.cos(ang), jnp.sin(ang)
    x1, x2 = x[..., :half], x[..., half:]
    return jnp.concatenate([x1 * cos - x2 * sin, x2 * cos + x1 * sin], axis=-1)


def short_conv_mixer(h, w_in, conv_w, conv_b, w_out):
    s = h.shape[1]
    proj = h @ w_in
    b_gate, c_gate, u, g = jnp.split(proj, 4, axis=-1)
    v = c_gate * u
    vp = jnp.pad(v, ((0, 0), (CONV_K - 1, 0), (0, 0)))
    conv = conv_b + vp[:, 0:s, :] * conv_w[0]
    for k in range(1, CONV_K):
        conv = conv + vp[:, k:k + s, :] * conv_w[k]
    y = b_gate * conv * jax.nn.silu(g)
    return y @ w_out


def retention_mixer(h, w_in, q_gain, k_gain, gn_w, gn_b, w_out):
    bsz, s, _ = h.shape
    proj = h @ w_in
    q, k, v, g = jnp.split(proj, [RET_QK_WIDTH, 2 * RET_QK_WIDTH, 2 * RET_QK_WIDTH + RET_WIDTH], axis=-1)
    pos = jnp.arange(s)
    q = rotary(rms_f32(split_heads(q, RET_HEADS, RET_QK_DIM), q_gain), pos)
    k = rotary(rms_f32(split_heads(k, RET_HEADS, RET_QK_DIM), k_gain), pos) * (RET_QK_DIM ** -0.5)
    v = split_heads(v, RET_HEADS, RET_V_DIM).astype(jnp.float32)

    log_g = jnp.log(1.0 - jnp.exp2(-5.0 - jnp.arange(RET_HEADS, dtype=jnp.float32)))
    c = RET_CHUNK
    n = s // c
    idx = jnp.arange(c, dtype=jnp.float32)
    diff = idx[:, None] - idx[None, :]
    decay_in = jnp.where(diff >= 0, jnp.exp(log_g[:, None, None] * jnp.maximum(diff, 0.0)), 0.0)
    decay_q = jnp.exp(log_g[:, None] * (idx + 1.0))[..., None]
    decay_k = jnp.exp(log_g[:, None] * (c - 1.0 - idx))[..., None]
    decay_chunk = jnp.exp(log_g * c)[:, None, None]

    def chunks(t):
        return t.reshape(bsz, RET_HEADS, n, c, t.shape[-1]).transpose(2, 0, 1, 3, 4)

    def step(state, inp):
        qi, ki, vi = inp
        inner = jnp.einsum('bhqd,bhkd->bhqk', qi, ki) * decay_in
        o = jnp.einsum('bhqk,bhkv->bhqv', inner, vi) + jnp.einsum('bhqd,bhdv->bhqv', qi, state) * decay_q
        state = state * decay_chunk + jnp.einsum('bhkd,bhkv->bhdv', ki * decay_k, vi)
        return state, o

    state0 = jnp.zeros((bsz, RET_HEADS, RET_QK_DIM, RET_V_DIM), jnp.float32)
    _, o = lax.scan(step, state0, (chunks(q), chunks(k), chunks(v)))
    o = o.transpose(1, 2, 0, 3, 4).reshape(bsz, RET_HEADS, s, RET_V_DIM)
    mu = jnp.mean(o, axis=-1, keepdims=True)
    var = jnp.mean(jnp.square(o - mu), axis=-1, keepdims=True)
    o = (o - mu) * lax.rsqrt(var + GN_EPS)
    o = o.transpose(0, 2, 1, 3).reshape(bsz, s, RET_WIDTH) * gn_w.astype(jnp.float32) + gn_b.astype(jnp.float32)
    y = jax.nn.silu(g) * o.astype(h.dtype)
    return y @ w_out


def stick_breaking_mixer(h, w_in, q_gain, k_gain, w_out):
    bsz, s, _ = h.shape
    proj = h @ w_in
    q, k, v, g = jnp.split(proj, 4, axis=-1)
    q = rms_f32(split_heads(q, SB_HEADS, SB_HEAD_DIM), q_gain) * (SB_HEAD_DIM ** -0.5)
    k = rms_f32(split_heads(k, SB_HEADS, SB_HEAD_DIM), k_gain)
    v = split_heads(v, SB_HEADS, SB_HEAD_DIM).astype(jnp.float32)
    nb = s // SB_BLOCK
    qb = q.reshape(bsz, SB_HEADS, nb, SB_BLOCK, SB_HEAD_DIM).transpose(2, 0, 1, 3, 4)
    kpos = jnp.arange(s)

    def block(args):
        qi, start = args
        z = jnp.einsum('bhqd,bhkd->bhqk', qi, k)
        qpos = start + jnp.arange(SB_BLOCK)
        mask = kpos[None, :] < qpos[:, None]
        log_beta = jax.nn.log_sigmoid(z)
        log_fail = jnp.where(mask, log_beta - z, 0.0)
        after = lax.cumsum(log_fail, axis=3, reverse=True) - log_fail
        w = jnp.where(mask, jnp.exp(log_beta + after), 0.0)
        return jnp.einsum('bhqk,bhkd->bhqd', w, v)

    o = lax.map(block, (qb, jnp.arange(nb) * SB_BLOCK))
    o = o.transpose(1, 0, 3, 2, 4).reshape(bsz, s, SB_WIDTH)
    y = jax.nn.silu(g) * o.astype(h.dtype)
    return y @ w_out


def setup_inputs(seed: int = 0) -> dict:
    key = jax.random.key(seed)
    ks = jax.random.split(key, 24)
    f32 = jnp.float32

    def nrm(k, shape, scale):
        return jax.random.normal(k, shape, f32) * scale

    def gain(k, shape):
        return 1.0 + 0.02 * jax.random.normal(k, shape, f32)

    out_scale = 0.5
    return {
        "x": jax.random.normal(ks[0], (BATCH, SEQ, D_MODEL), f32),
        "conv_norm": gain(ks[1], (N_CONV, D_MODEL)),
        "conv_w_in": nrm(ks[2], (N_CONV, D_MODEL, 4 * CONV_WIDTH), D_MODEL ** -0.5),
        "conv_w": nrm(ks[3], (N_CONV, CONV_K, CONV_WIDTH), CONV_K ** -0.5),
        "conv_b": nrm(ks[4], (N_CONV, CONV_WIDTH), 0.02),
        "conv_w_out": nrm(ks[5], (N_CONV, CONV_WIDTH, D_MODEL), out_scale * CONV_WIDTH ** -0.5),
        "ret_norm": gain(ks[6], (N_RET, D_MODEL)),
        "ret_w_in": nrm(ks[7], (N_RET, D_MODEL, 2 * RET_QK_WIDTH + 2 * RET_WIDTH), D_MODEL ** -0.5),
        "ret_q_gain": gain(ks[8], (N_RET, RET_QK_DIM)),
        "ret_k_gain": gain(ks[9], (N_RET, RET_QK_DIM)),
        "ret_gn_w": gain(ks[10], (N_RET, RET_WIDTH)),
        "ret_gn_b": nrm(ks[11], (N_RET, RET_WIDTH), 0.02),
        "ret_w_out": nrm(ks[12], (N_RET, RET_WIDTH, D_MODEL), out_scale * RET_WIDTH ** -0.5),
        "sb_norm": gain(ks[13], (N_SB, D_MODEL)),
        "sb_w_in": nrm(ks[14], (N_SB, D_MODEL, 4 * SB_WIDTH), D_MODEL ** -0.5),
        "sb_q_gain": gain(ks[15], (N_SB, SB_HEAD_DIM)),
        "sb_k_gain": gain(ks[16], (N_SB, SB_HEAD_DIM)),
        "sb_w_out": nrm(ks[17], (N_SB, SB_WIDTH, D_MODEL), out_scale * SB_WIDTH ** -0.5),
    }


def reference(x, conv_norm, conv_w_in, conv_w, conv_b, conv_w_out,
              ret_norm, ret_w_in, ret_q_gain, ret_k_gain, ret_gn_w, ret_gn_b, ret_w_out,
              sb_norm, sb_w_in, sb_q_gain, sb_k_gain, sb_w_out):
    for i in range(DEPTH):
        kind = i % N_MIXERS
        j = i // N_MIXERS
        if kind == 0:
            h = rmsnorm(x, conv_norm[j])
            x = x + short_conv_mixer(h, conv_w_in[j], conv_w[j], conv_b[j], conv_w_out[j])
        elif kind == 1:
            h = rmsnorm(x, ret_norm[j])
            x = x + retention_mixer(h, ret_w_in[j], ret_q_gain[j], ret_k_gain[j],
                                    ret_gn_w[j], ret_gn_b[j], ret_w_out[j])
        else:
            h = rmsnorm(x, sb_norm[j])
            x = x + stick_breaking_mixer(h, sb_w_in[j], sb_q_gain[j], sb_k_gain[j], sb_w_out[j])
    return x
```

```python
import functools
import math

import jax
import jax.numpy as jnp
from jax import lax
from jax.experimental import pallas as pl
from jax.experimental.pallas import tpu as pltpu

EPS = 1e-6
GN_EPS = 1e-5
N_MIXERS = 3
CONV_K = 3
RET_QK_DIM = 256
RET_V_DIM = 512
RET_CHUNK = 128
ROPE_BASE = 10000.0
SB_HEAD_DIM = 128

F32 = jnp.float32
BF16 = jnp.bfloat16

VMEM_LIMIT_BYTES = 56 * 1024 * 1024


def _params(*sem):
    return pltpu.CompilerParams(dimension_semantics=sem, vmem_limit_bytes=VMEM_LIMIT_BYTES)


def _silu(g):
    return g * jax.nn.sigmoid(g)


def _norm_proj_kernel(x_ref, g_ref, w_ref, o_ref, h_ref, *, rows):
    @pl.when(pl.program_id(1) == 0)
    def _():
        def body(r, c):
            sl = pl.ds(pl.multiple_of(r * rows, rows), rows)
            x = x_ref[sl, :]
            ms = jnp.mean(x * x, axis=-1, keepdims=True)
            h_ref[sl, :] = (x * lax.rsqrt(ms + EPS) * g_ref[...]).astype(h_ref.dtype)
            return c

        lax.fori_loop(0, x_ref.shape[0] // rows, body, 0)

    o_ref[...] = jnp.dot(h_ref[...], w_ref[...], preferred_element_type=F32).astype(o_ref.dtype)


def norm_proj(x2, gain, w, *, tm, tn):
    t, d = x2.shape
    n = w.shape[1]
    tm, tn = min(tm, t), min(tn, n)
    return pl.pallas_call(
        functools.partial(_norm_proj_kernel, rows=32),
        grid=(t // tm, n // tn),
        in_specs=[
            pl.BlockSpec((tm, d), lambda i, j: (i, 0)),
            pl.BlockSpec((1, d), lambda i, j: (0, 0)),
            pl.BlockSpec((d, tn), lambda i, j: (0, j)),
        ],
        out_specs=pl.BlockSpec((tm, tn), lambda i, j: (i, j)),
        out_shape=jax.ShapeDtypeStruct((t, n), BF16),
        scratch_shapes=[pltpu.VMEM((tm, d), BF16)],
        compiler_params=_params("parallel", "arbitrary"),
        name="norm_proj",
    )(x2, gain.reshape(1, d), w)


def _out_proj_kernel(y_ref, w_ref, x_ref, o_ref):
    o_ref[...] = x_ref[...] + jnp.dot(y_ref[...], w_ref[...], preferred_element_type=F32)


def out_proj(y2, w, x2, *, tm, tn):
    t, k = y2.shape
    d = w.shape[1]
    tm, tn = min(tm, t), min(tn, d)
    return pl.pallas_call(
        _out_proj_kernel,
        grid=(d // tn, t // tm),
        in_specs=[
            pl.BlockSpec((tm, k), lambda j, i: (i, 0)),
            pl.BlockSpec((k, tn), lambda j, i: (0, j)),
            pl.BlockSpec((tm, tn), lambda j, i: (i, j)),
        ],
        out_specs=pl.BlockSpec((tm, tn), lambda j, i: (i, j)),
        out_shape=jax.ShapeDtypeStruct((t, d), F32),
        compiler_params=_params("parallel", "parallel"),
        name="out_proj",
    )(y2, w, x2)


def _conv_core_kernel(b_ref, c_ref, u_ref, g_ref, cw_ref, cb_ref, y_ref):
    v = c_ref[0].astype(F32) * u_ref[0].astype(F32)
    row = lax.broadcasted_iota(jnp.int32, v.shape, 0)
    v1 = jnp.where(row >= 1, pltpu.roll(v, 1, 0), 0.0)
    v2 = jnp.where(row >= 2, pltpu.roll(v, 2, 0), 0.0)
    conv = cb_ref[...] + v2 * cw_ref[0:1, :]
    conv = conv + v1 * cw_ref[1:2, :]
    conv = conv + v * cw_ref[2:3, :]
    y = b_ref[0].astype(F32) * conv * _silu(g_ref[0].astype(F32))
    y_ref[0] = y.astype(y_ref.dtype)


def conv_core(p3, conv_w, conv_b, *, ec):
    bsz, s, n4 = p3.shape
    e = n4 // 4
    ec = min(ec, e)
    nj = e // ec

    def stream(g):
        return pl.BlockSpec((1, s, ec), lambda b, j: (b, 0, g * nj + j))

    return pl.pallas_call(
        _conv_core_kernel,
        grid=(bsz, nj),
        in_specs=[stream(0), stream(1), stream(2), stream(3),
                  pl.BlockSpec((CONV_K, ec), lambda b, j: (0, j)),
                  pl.BlockSpec((1, ec), lambda b, j: (0, j))],
        out_specs=pl.BlockSpec((1, s, ec), lambda b, j: (b, 0, j)),
        out_shape=jax.ShapeDtypeStruct((bsz, s, e), BF16),
        compiler_params=_params("parallel", "parallel"),
        name="conv_core",
    )(p3, p3, p3, p3, conv_w, conv_b.reshape(1, e))


def _ret_core_kernel(q_ref, k_ref, v_ref, g_ref, qg_ref, kg_ref, cos_ref, sin_ref,
                     din_ref, dq_ref, dk_ref, gw_ref, gb_ref, y_ref, state_ref):
    c = RET_CHUNK
    half = RET_QK_DIM // 2

    @pl.when(pl.program_id(2) == 0)
    def _():
        state_ref[...] = jnp.zeros_like(state_ref)

    cos, sin = cos_ref[...], sin_ref[...]

    def norm_rot(t_ref, gain):
        t = t_ref[0].astype(F32)
        t = t * lax.rsqrt(jnp.mean(t * t, axis=-1, keepdims=True) + EPS) * gain
        t1, t2 = t[:, :half], t[:, half:]
        return jnp.concatenate([t1 * cos - t2 * sin, t2 * cos + t1 * sin], axis=-1)

    q = norm_rot(q_ref, qg_ref[...])
    k = norm_rot(k_ref, kg_ref[...]) * (RET_QK_DIM ** -0.5)

    din = din_ref[0]
    dq = dq_ref[0]
    dk = dk_ref[0]
    dchunk = dq[c - 1:c, :]

    for ci in range(q.shape[0] // c):
        sl = slice(ci * c, (ci + 1) * c)
        qi = q[sl].astype(BF16)
        ki = k[sl]
        vi = v_ref[0, sl, :]
        st = state_ref[...]
        inner = lax.dot_general(qi, ki.astype(BF16), (((1,), (1,)), ((), ())),
                                preferred_element_type=F32) * din
        o = jnp.dot(inner.astype(BF16), vi, preferred_element_type=F32)
        o = o + jnp.dot(qi, st.astype(BF16), preferred_element_type=F32) * dq
        kd = (ki * dk).astype(BF16)
        state_ref[...] = st * dchunk + lax.dot_general(
            kd, vi, (((0,), (0,)), ((), ())), preferred_element_type=F32)
        mu = jnp.mean(o, axis=-1, keepdims=True)
        var = jnp.mean(jnp.square(o - mu), axis=-1, keepdims=True)
        on = (o - mu) * lax.rsqrt(var + GN_EPS)
        on = on * gw_ref[...] + gb_ref[...]
        y = _silu(g_ref[0, sl, :].astype(F32)) * on
        y_ref[0, sl, :] = y.astype(y_ref.dtype)


def _ret_tables(s, heads):
    c = RET_CHUNK
    half = RET_QK_DIM // 2
    inv_freq = ROPE_BASE ** (-jnp.arange(half, dtype=F32) / half)
    ang = jnp.arange(s).astype(F32)[:, None] * inv_freq[None, :]
    log_g = jnp.log(1.0 - jnp.exp2(-5.0 - jnp.arange(heads, dtype=F32)))
    idx = jnp.arange(c, dtype=F32)
    diff = idx[:, None] - idx[None, :]
    din = jnp.where(diff >= 0, jnp.exp(log_g[:, None, None] * jnp.maximum(diff, 0.0)), 0.0)
    dq = jnp.exp(log_g[:, None] * (idx + 1.0))[..., None]
    dk = jnp.exp(log_g[:, None] * (c - 1.0 - idx))[..., None]
    return jnp.cos(ang), jnp.sin(ang), din, dq, dk


def ret_core(p3, q_gain, k_gain, gn_w, gn_b, *, ts):
    bsz, s, n = p3.shape
    heads = n // (2 * RET_QK_DIM + 2 * RET_V_DIM)
    ts = min(ts, s)
    c = RET_CHUNK
    dqk, dv = RET_QK_DIM, RET_V_DIM
    cos, sin, din, dq, dk = _ret_tables(s, heads)
    v_off = 2 * heads * dqk // dv
    g_off = v_off + heads
    return pl.pallas_call(
        _ret_core_kernel,
        grid=(bsz, heads, s // ts),
        in_specs=[
            pl.BlockSpec((1, ts, dqk), lambda b, h, t: (b, t, h)),
            pl.BlockSpec((1, ts, dqk), lambda b, h, t: (b, t, heads + h)),
            pl.BlockSpec((1, ts, dv), lambda b, h, t: (b, t, v_off + h)),
            pl.BlockSpec((1, ts, dv), lambda b, h, t: (b, t, g_off + h)),
            pl.BlockSpec((1, dqk), lambda b, h, t: (0, 0)),
            pl.BlockSpec((1, dqk), lambda b, h, t: (0, 0)),
            pl.BlockSpec((ts, dqk // 2), lambda b, h, t: (t, 0)),
            pl.BlockSpec((ts, dqk // 2), lambda b, h, t: (t, 0)),
            pl.BlockSpec((1, c, c), lambda b, h, t: (h, 0, 0)),
            pl.BlockSpec((1, c, 1), lambda b, h, t: (h, 0, 0)),
            pl.BlockSpec((1, c, 1), lambda b, h, t: (h, 0, 0)),
            pl.BlockSpec((1, dv), lambda b, h, t: (0, h)),
            pl.BlockSpec((1, dv), lambda b, h, t: (0, h)),
        ],
        out_specs=pl.BlockSpec((1, ts, dv), lambda b, h, t: (b, t, h)),
        out_shape=jax.ShapeDtypeStruct((bsz, s, heads * dv), BF16),
        scratch_shapes=[pltpu.VMEM((dqk, dv), F32)],
        compiler_params=_params("parallel", "parallel", "arbitrary"),
        name="ret_core",
    )(p3, p3, p3, p3, q_gain.reshape(1, dqk), k_gain.reshape(1, dqk), cos, sin, din, dq, dk,
      gn_w.reshape(1, heads * dv), gn_b.reshape(1, heads * dv))


def _sb_core_kernel(q_ref, k_ref, v_ref, g_ref, qg_ref, kg_ref, u_ref, y_ref,
                    qs_ref, ks_ref, acc_ref, run_ref, *, tb, hp):
    d = SB_HEAD_DIM
    s = q_ref.shape[1]
    nt = s // tb
    row = lax.broadcasted_iota(jnp.int32, (tb, tb), 0)
    col = lax.broadcasted_iota(jnp.int32, (tb, tb), 1)
    causal = col < row

    def rms(t, gain):
        return t * lax.rsqrt(jnp.mean(t * t, axis=-1, keepdims=True) + EPS) * gain

    def tile(qi, j0, masked):
        kj = ks_ref[pl.ds(j0, tb), :]
        z = lax.dot_general(qi, kj, (((1,), (1,)), ((), ())), preferred_element_type=F32)
        sp = jnp.maximum(z, 0.0) + jnp.log1p(jnp.exp(-jnp.abs(z)))
        if masked:
            sp = jnp.where(causal, sp, 0.0)
        hi = sp.astype(BF16)
        lo = (sp - hi.astype(F32)).astype(BF16)
        suf = jnp.dot(hi, u_ref[...], preferred_element_type=F32)
        suf = suf + jnp.dot(lo, u_ref[...], preferred_element_type=F32)
        run = run_ref[...]
        w = jnp.exp(z - sp - (suf + run))
        if masked:
            w = jnp.where(causal, w, 0.0)
        vj = v_ref[0, pl.ds(j0, tb), cs]
        acc_ref[...] += jnp.dot(w.astype(BF16), vj, preferred_element_type=F32)
        run_ref[...] = run + suf[:, 0:1] + sp[:, 0:1]

    for hh in range(hp):
        cs = slice(hh * d, (hh + 1) * d)
        qs_ref[...] = (rms(q_ref[0, :, cs].astype(F32), qg_ref[...]) * (d ** -0.5)).astype(BF16)
        ks_ref[...] = rms(k_ref[0, :, cs].astype(F32), kg_ref[...]).astype(BF16)

        def qtile(i, carry):
            i0 = pl.multiple_of(i * tb, tb)
            qi = qs_ref[pl.ds(i0, tb), :]
            acc_ref[...] = jnp.zeros_like(acc_ref)
            run_ref[...] = jnp.zeros_like(run_ref)
            tile(qi, i0, True)

            def ktile(jj, c2):
                tile(qi, pl.multiple_of((i - 1 - jj) * tb, tb), False)
                return c2

            lax.fori_loop(0, i, ktile, 0)
            g = g_ref[0, pl.ds(i0, tb), cs].astype(F32)
            y_ref[0, pl.ds(i0, tb), cs] = (_silu(g) * acc_ref[...]).astype(y_ref.dtype)
            return carry

        lax.fori_loop(0, nt, qtile, 0)


def sb_core(p3, q_gain, k_gain, *, tb, hp):
    bsz, s, n4 = p3.shape
    d = SB_HEAD_DIM
    width = n4 // 4
    heads = width // d
    hp = min(hp, heads)
    tb = min(tb, s)
    ng = heads // hp
    u = (jnp.arange(tb)[:, None] > jnp.arange(tb)[None, :]).astype(BF16)

    def stream(g):
        return pl.BlockSpec((1, s, hp * d), lambda b, h: (b, 0, g * ng + h))

    return pl.pallas_call(
        functools.partial(_sb_core_kernel, tb=tb, hp=hp),
        grid=(bsz, ng),
        in_specs=[stream(0), stream(1), stream(2), stream(3),
                  pl.BlockSpec((1, d), lambda b, h: (0, 0)),
                  pl.BlockSpec((1, d), lambda b, h: (0, 0)),
                  pl.BlockSpec((tb, tb), lambda b, h: (0, 0))],
        out_specs=pl.BlockSpec((1, s, hp * d), lambda b, h: (b, 0, h)),
        out_shape=jax.ShapeDtypeStruct((bsz, s, width), BF16),
        scratch_shapes=[pltpu.VMEM((s, d), BF16), pltpu.VMEM((s, d), BF16),
                        pltpu.VMEM((tb, d), F32), pltpu.VMEM((tb, 1), F32)],
        compiler_params=_params("parallel", "parallel"),
        name="sb_core",
    )(p3, p3, p3, p3, q_gain.reshape(1, d), k_gain.reshape(1, d), u)


def kernel(x, conv_norm, conv_w_in, conv_w, conv_b, conv_w_out, ret_norm, ret_w_in, ret_q_gain, ret_k_gain,
           ret_gn_w, ret_gn_b, ret_w_out, sb_norm, sb_w_in, sb_q_gain, sb_k_gain, sb_w_out):
    bsz, s, d = x.shape
    depth = conv_norm.shape[0] + ret_norm.shape[0] + sb_norm.shape[0]
    x2 = x.reshape(bsz * s, d)
    for i in range(depth):
        kind, j = i % N_MIXERS, i // N_MIXERS
        if kind == 0:
            p = norm_proj(x2, conv_norm[j], conv_w_in[j].astype(BF16), tm=1024, tn=1024)
            y = conv_core(p.reshape(bsz, s, -1), conv_w[j], conv_b[j], ec=256)
            w_out = conv_w_out[j]
        elif kind == 1:
            p = norm_proj(x2, ret_norm[j], ret_w_in[j].astype(BF16), tm=1024, tn=1024)
            y = ret_core(p.reshape(bsz, s, -1), ret_q_gain[j], ret_k_gain[j], ret_gn_w[j], ret_gn_b[j], ts=512)
            w_out = ret_w_out[j]
        else:
            p = norm_proj(x2, sb_norm[j], sb_w_in[j].astype(BF16), tm=1024, tn=1024)
            y = sb_core(p.reshape(bsz, s, -1), sb_q_gain[j], sb_k_gain[j], tb=256, hp=2)
            w_out = sb_w_out[j]
        x2 = out_proj(y.reshape(bsz * s, -1), w_out.astype(BF16), x2, tm=1024, tn=1024)
    return x2.reshape(bsz, s, d)
```

```python
import functools
import math

import jax
import jax.numpy as jnp
from jax import lax
from jax.experimental import pallas as pl
from jax.experimental.pallas import tpu as pltpu

EPS = 1e-6
GN_EPS = 1e-5
N_MIXERS = 3
CONV_K = 3
RET_QK_DIM = 256
RET_V_DIM = 512
RET_CHUNK = 128
ROPE_BASE = 10000.0
SB_HEAD_DIM = 128
SB_UNDERFLOW = 104.0

F32 = jnp.float32
BF16 = jnp.bfloat16

VMEM_LIMIT_BYTES = 56 * 1024 * 1024


def _params(*sem):
    return pltpu.CompilerParams(dimension_semantics=sem, vmem_limit_bytes=VMEM_LIMIT_BYTES)


def _silu(g):
    return g * jax.nn.sigmoid(g)


def _norm_proj_kernel(x_ref, g_ref, w_ref, o_ref, h_ref, *, rows):
    @pl.when(pl.program_id(1) == 0)
    def _():
        def body(r, c):
            sl = pl.ds(pl.multiple_of(r * rows, rows), rows)
            x = x_ref[sl, :]
            ms = jnp.mean(x * x, axis=-1, keepdims=True)
            h_ref[sl, :] = (x * lax.rsqrt(ms + EPS) * g_ref[...]).astype(h_ref.dtype)
            return c

        lax.fori_loop(0, x_ref.shape[0] // rows, body, 0)

    o_ref[...] = jnp.dot(h_ref[...], w_ref[...], preferred_element_type=F32).astype(o_ref.dtype)


def norm_proj(x2, gain, w, *, tm, tn):
    t, d = x2.shape
    n = w.shape[1]
    tm, tn = min(tm, t), min(tn, n)
    return pl.pallas_call(
        functools.partial(_norm_proj_kernel, rows=32),
        grid=(t // tm, n // tn),
        in_specs=[
            pl.BlockSpec((tm, d), lambda i, j: (i, 0)),
            pl.BlockSpec((1, d), lambda i, j: (0, 0)),
            pl.BlockSpec((d, tn), lambda i, j: (0, j)),
        ],
        out_specs=pl.BlockSpec((tm, tn), lambda i, j: (i, j)),
        out_shape=jax.ShapeDtypeStruct((t, n), BF16),
        scratch_shapes=[pltpu.VMEM((tm, d), BF16)],
        compiler_params=_params("parallel", "arbitrary"),
        name="norm_proj",
    )(x2, gain.reshape(1, d), w)


def _out_proj_kernel(y_ref, w_ref, x_ref, o_ref):
    o_ref[...] = x_ref[...] + jnp.dot(y_ref[...], w_ref[...], preferred_element_type=F32)


def out_proj(y2, w, x2, *, tm, tn):
    t, k = y2.shape
    d = w.shape[1]
    tm, tn = min(tm, t), min(tn, d)
    return pl.pallas_call(
        _out_proj_kernel,
        grid=(d // tn, t // tm),
        in_specs=[
            pl.BlockSpec((tm, k), lambda j, i: (i, 0)),
            pl.BlockSpec((k, tn), lambda j, i: (0, j)),
            pl.BlockSpec((tm, tn), lambda j, i: (i, j)),
        ],
        out_specs=pl.BlockSpec((tm, tn), lambda j, i: (i, j)),
        out_shape=jax.ShapeDtypeStruct((t, d), F32),
        compiler_params=_params("parallel", "parallel"),
        name="out_proj",
    )(y2, w, x2)


def _conv_core_kernel(b_ref, c_ref, u_ref, g_ref, cw_ref, cb_ref, y_ref):
    v = c_ref[0].astype(F32) * u_ref[0].astype(F32)
    row = lax.broadcasted_iota(jnp.int32, v.shape, 0)
    v1 = jnp.where(row >= 1, pltpu.roll(v, 1, 0), 0.0)
    v2 = jnp.where(row >= 2, pltpu.roll(v, 2, 0), 0.0)
    conv = cb_ref[...] + v2 * cw_ref[0:1, :]
    conv = conv + v1 * cw_ref[1:2, :]
    conv = conv + v * cw_ref[2:3, :]
    y = b_ref[0].astype(F32) * conv * _silu(g_ref[0].astype(F32))
    y_ref[0] = y.astype(y_ref.dtype)


def conv_core(p3, conv_w, conv_b, *, ec):
    bsz, s, n4 = p3.shape
    e = n4 // 4
    ec = min(ec, e)
    nj = e // ec

    def stream(g):
        return pl.BlockSpec((1, s, ec), lambda b, j: (b, 0, g * nj + j))

    return pl.pallas_call(
        _conv_core_kernel,
        grid=(bsz, nj),
        in_specs=[stream(0), stream(1), stream(2), stream(3),
                  pl.BlockSpec((CONV_K, ec), lambda b, j: (0, j)),
                  pl.BlockSpec((1, ec), lambda b, j: (0, j))],
        out_specs=pl.BlockSpec((1, s, ec), lambda b, j: (b, 0, j)),
        out_shape=jax.ShapeDtypeStruct((bsz, s, e), BF16),
        compiler_params=_params("parallel", "parallel"),
        name="conv_core",
    )(p3, p3, p3, p3, conv_w, conv_b.reshape(1, e))


def _ret_core_kernel(q_ref, k_ref, v_ref, g_ref, qg_ref, kg_ref, cos_ref, sin_ref,
                     din_ref, dq_ref, dk_ref, gw_ref, gb_ref, y_ref, state_ref):
    c = RET_CHUNK
    half = RET_QK_DIM // 2

    @pl.when(pl.program_id(2) == 0)
    def _():
        state_ref[...] = jnp.zeros_like(state_ref)

    cos, sin = cos_ref[...], sin_ref[...]

    def norm_rot(t_ref, gain):
        t = t_ref[0].astype(F32)
        t = t * lax.rsqrt(jnp.mean(t * t, axis=-1, keepdims=True) + EPS) * gain
        t1, t2 = t[:, :half], t[:, half:]
        return jnp.concatenate([t1 * cos - t2 * sin, t2 * cos + t1 * sin], axis=-1)

    q = norm_rot(q_ref, qg_ref[...])
    k = norm_rot(k_ref, kg_ref[...]) * (RET_QK_DIM ** -0.5)

    din = din_ref[0]
    dq = dq_ref[0]
    dk = dk_ref[0]
    dchunk = dq[c - 1:c, :]

    for ci in range(q.shape[0] // c):
        sl = slice(ci * c, (ci + 1) * c)
        qi = q[sl].astype(BF16)
        ki = k[sl]
        vi = v_ref[0, sl, :]
        st = state_ref[...]
        inner = lax.dot_general(qi, ki.astype(BF16), (((1,), (1,)), ((), ())),
                                preferred_element_type=F32) * din
        o = jnp.dot(inner.astype(BF16), vi, preferred_element_type=F32)
        o = o + jnp.dot(qi, st.astype(BF16), preferred_element_type=F32) * dq
        kd = (ki * dk).astype(BF16)
        state_ref[...] = st * dchunk + lax.dot_general(
            kd, vi, (((0,), (0,)), ((), ())), preferred_element_type=F32)
        mu = jnp.mean(o, axis=-1, keepdims=True)
        var = jnp.mean(jnp.square(o - mu), axis=-1, keepdims=True)
        on = (o - mu) * lax.rsqrt(var + GN_EPS)
        on = on * gw_ref[...] + gb_ref[...]
        y = _silu(g_ref[0, sl, :].astype(F32)) * on
        y_ref[0, sl, :] = y.astype(y_ref.dtype)


def _ret_tables(s, heads):
    c = RET_CHUNK
    half = RET_QK_DIM // 2
    inv_freq = ROPE_BASE ** (-jnp.arange(half, dtype=F32) / half)
    ang = jnp.arange(s).astype(F32)[:, None] * inv_freq[None, :]
    log_g = jnp.log(1.0 - jnp.exp2(-5.0 - jnp.arange(heads, dtype=F32)))
    idx = jnp.arange(c, dtype=F32)
    diff = idx[:, None] - idx[None, :]
    din = jnp.where(diff >= 0, jnp.exp(log_g[:, None, None] * jnp.maximum(diff, 0.0)), 0.0)
    dq = jnp.exp(log_g[:, None] * (idx + 1.0))[..., None]
    dk = jnp.exp(log_g[:, None] * (c - 1.0 - idx))[..., None]
    return jnp.cos(ang), jnp.sin(ang), din, dq, dk


def ret_core(p3, q_gain, k_gain, gn_w, gn_b, *, ts):
    bsz, s, n = p3.shape
    heads = n // (2 * RET_QK_DIM + 2 * RET_V_DIM)
    ts = min(ts, s)
    c = RET_CHUNK
    dqk, dv = RET_QK_DIM, RET_V_DIM
    cos, sin, din, dq, dk = _ret_tables(s, heads)
    v_off = 2 * heads * dqk // dv
    g_off = v_off + heads
    return pl.pallas_call(
        _ret_core_kernel,
        grid=(bsz, heads, s // ts),
        in_specs=[
            pl.BlockSpec((1, ts, dqk), lambda b, h, t: (b, t, h)),
            pl.BlockSpec((1, ts, dqk), lambda b, h, t: (b, t, heads + h)),
            pl.BlockSpec((1, ts, dv), lambda b, h, t: (b, t, v_off + h)),
            pl.BlockSpec((1, ts, dv), lambda b, h, t: (b, t, g_off + h)),
            pl.BlockSpec((1, dqk), lambda b, h, t: (0, 0)),
            pl.BlockSpec((1, dqk), lambda b, h, t: (0, 0)),
            pl.BlockSpec((ts, dqk // 2), lambda b, h, t: (t, 0)),
            pl.BlockSpec((ts, dqk // 2), lambda b, h, t: (t, 0)),
            pl.BlockSpec((1, c, c), lambda b, h, t: (h, 0, 0)),
            pl.BlockSpec((1, c, 1), lambda b, h, t: (h, 0, 0)),
            pl.BlockSpec((1, c, 1), lambda b, h, t: (h, 0, 0)),
            pl.BlockSpec((1, dv), lambda b, h, t: (0, h)),
            pl.BlockSpec((1, dv), lambda b, h, t: (0, h)),
        ],
        out_specs=pl.BlockSpec((1, ts, dv), lambda b, h, t: (b, t, h)),
        out_shape=jax.ShapeDtypeStruct((bsz, s, heads * dv), BF16),
        scratch_shapes=[pltpu.VMEM((dqk, dv), F32)],
        compiler_params=_params("parallel", "parallel", "arbitrary"),
        name="ret_core",
    )(p3, p3, p3, p3, q_gain.reshape(1, dqk), k_gain.reshape(1, dqk), cos, sin, din, dq, dk,
      gn_w.reshape(1, heads * dv), gn_b.reshape(1, heads * dv))


def _sb_core_kernel(q_ref, k_ref, v_ref, g_ref, qg_ref, kg_ref, u_ref, y_ref,
                    qs_ref, ks_ref, acc_ref, run_ref, *, tb, hp):
    d = SB_HEAD_DIM
    s = q_ref.shape[1]
    nt = s // tb
    row = lax.broadcasted_iota(jnp.int32, (tb, tb), 0)
    col = lax.broadcasted_iota(jnp.int32, (tb, tb), 1)
    causal = col < row

    def rms(t, gain):
        return t * lax.rsqrt(jnp.mean(t * t, axis=-1, keepdims=True) + EPS) * gain

    def scores(qi, hh, j0):
        kj = ks_ref[hh, pl.ds(j0, tb), :]
        return lax.dot_general(qi, kj, (((1,), (1,)), ((), ())), preferred_element_type=F32)

    def softplus(z):
        return jnp.maximum(z, 0.0) + jnp.log(1.0 + jnp.exp(-jnp.abs(z)))

    def suffix(sp):
        hi = sp.astype(BF16)
        lo = (sp - hi.astype(F32)).astype(BF16)
        return (jnp.dot(hi, u_ref[...], preferred_element_type=F32)
                + jnp.dot(lo, u_ref[...], preferred_element_type=F32))

    def values(hh, j0):
        return v_ref[0, pl.ds(j0, tb), hh * d:(hh + 1) * d]

    def diag_tile(qi, hh, i0):
        z = scores(qi, hh, i0)
        sp = jnp.where(causal, softplus(z), 0.0)
        suf = suffix(sp)
        w = jnp.where(causal, jnp.exp(z - sp - suf), 0.0)
        acc = jnp.dot(w.astype(BF16), values(hh, i0), preferred_element_type=F32)
        return acc, suf[:, 0:1] + sp[:, 0:1]

    def off_tile(qi, hh, j0, acc, run):
        z = scores(qi, hh, j0)
        sp = softplus(z)
        suf = suffix(sp)
        w = jnp.exp(z - sp - (suf + run))
        acc = acc + jnp.dot(w.astype(BF16), values(hh, j0), preferred_element_type=F32)
        return acc, run + suf[:, 0:1] + sp[:, 0:1]

    def finish(hh, i0, acc):
        g = g_ref[0, pl.ds(i0, tb), hh * d:(hh + 1) * d].astype(F32)
        y_ref[0, pl.ds(i0, tb), hh * d:(hh + 1) * d] = (_silu(g) * acc).astype(y_ref.dtype)

    for hh in range(hp):
        cs = slice(hh * d, (hh + 1) * d)
        qs_ref[hh] = (rms(q_ref[0, :, cs].astype(F32), qg_ref[...]) * (d ** -0.5)).astype(BF16)
        ks_ref[hh] = rms(k_ref[0, :, cs].astype(F32), kg_ref[...]).astype(BF16)

    for hh in range(hp):
        acc, _ = diag_tile(qs_ref[hh, 0:tb, :], hh, 0)
        finish(hh, 0, acc)

    def qtile(i, carry):
        i0 = pl.multiple_of(i * tb, tb)
        for hh in range(hp):
            qi = qs_ref[hh, pl.ds(i0, tb), :]
            acc, run = diag_tile(qi, hh, i0)
            acc, run = off_tile(qi, hh, pl.multiple_of(i0 - tb, tb), acc, run)
            acc_ref[hh] = acc
            run_ref[hh] = run

            def more(c):
                j, rmin = c
                return jnp.logical_and(j >= 0, rmin <= SB_UNDERFLOW)

            def ktile(c, qi=qi, hh=hh):
                j, _ = c
                a, r = off_tile(qi, hh, pl.multiple_of(j * tb, tb), acc_ref[hh], run_ref[hh])
                acc_ref[hh] = a
                run_ref[hh] = r
                return j - 1, jnp.min(r)

            lax.while_loop(more, ktile, (i - 2, jnp.min(run)))
            finish(hh, i0, acc_ref[hh])
        return carry

    lax.fori_loop(1, nt, qtile, 0)


def sb_core(p3, q_gain, k_gain, *, tb, hp):
    bsz, s, n4 = p3.shape
    d = SB_HEAD_DIM
    width = n4 // 4
    heads = width // d
    hp = min(hp, heads)
    tb = min(tb, s)
    ng = heads // hp
    u = (jnp.arange(tb)[:, None] > jnp.arange(tb)[None, :]).astype(BF16)

    def stream(g):
        return pl.BlockSpec((1, s, hp * d), lambda b, h: (b, 0, g * ng + h))

    return pl.pallas_call(
        functools.partial(_sb_core_kernel, tb=tb, hp=hp),
        grid=(bsz, ng),
        in_specs=[stream(0), stream(1), stream(2), stream(3),
                  pl.BlockSpec((1, d), lambda b, h: (0, 0)),
                  pl.BlockSpec((1, d), lambda b, h: (0, 0)),
                  pl.BlockSpec((tb, tb), lambda b, h: (0, 0))],
        out_specs=pl.BlockSpec((1, s, hp * d), lambda b, h: (b, 0, h)),
        out_shape=jax.ShapeDtypeStruct((bsz, s, width), BF16),
        scratch_shapes=[pltpu.VMEM((hp, s, d), BF16), pltpu.VMEM((hp, s, d), BF16),
                        pltpu.VMEM((hp, tb, d), F32), pltpu.VMEM((hp, tb, 1), F32)],
        compiler_params=_params("parallel", "parallel"),
        name="sb_core",
    )(p3, p3, p3, p3, q_gain.reshape(1, d), k_gain.reshape(1, d), u)


def kernel(x, conv_norm, conv_w_in, conv_w, conv_b, conv_w_out, ret_norm, ret_w_in, ret_q_gain, ret_k_gain,
           ret_gn_w, ret_gn_b, ret_w_out, sb_norm, sb_w_in, sb_q_gain, sb_k_gain, sb_w_out):
    bsz, s, d = x.shape
    depth = conv_norm.shape[0] + ret_norm.shape[0] + sb_norm.shape[0]
    x2 = x.reshape(bsz * s, d)
    for i in range(depth):
        kind, j = i % N_MIXERS, i // N_MIXERS
        if kind == 0:
            p = norm_proj(x2, conv_norm[j], conv_w_in[j].astype(BF16), tm=1024, tn=1024)
            y = conv_core(p.reshape(bsz, s, -1), conv_w[j], conv_b[j], ec=256)
            w_out = conv_w_out[j]
        elif kind == 1:
            p = norm_proj(x2, ret_norm[j], ret_w_in[j].astype(BF16), tm=1024, tn=1024)
            y = ret_core(p.reshape(bsz, s, -1), ret_q_gain[j], ret_k_gain[j], ret_gn_w[j], ret_gn_b[j], ts=512)
            w_out = ret_w_out[j]
        else:
            p = norm_proj(x2, sb_norm[j], sb_w_in[j].astype(BF16), tm=1024, tn=1024)
            y = sb_core(p.reshape(bsz, s, -1), sb_q_gain[j], sb_k_gain[j], tb=256, hp=2)
            w_out = sb_w_out[j]
        x2 = out_proj(y.reshape(bsz * s, -1), w_out.astype(BF16), x2, tm=1024, tn=1024)
    return x2.reshape(bsz, s, d)
```

```python
import functools
import math

import jax
import jax.numpy as jnp
from jax import lax
from jax.experimental import pallas as pl
from jax.experimental.pallas import tpu as pltpu

EPS = 1e-6
GN_EPS = 1e-5
N_MIXERS = 3
CONV_K = 3
RET_QK_DIM = 256
RET_V_DIM = 512
RET_CHUNK = 128
ROPE_BASE = 10000.0
SB_HEAD_DIM = 128
SB_UNDERFLOW = 104.0

F32 = jnp.float32
BF16 = jnp.bfloat16

VMEM_LIMIT_BYTES = 56 * 1024 * 1024


def _params(*sem):
    return pltpu.CompilerParams(dimension_semantics=sem, vmem_limit_bytes=VMEM_LIMIT_BYTES)


def _silu(g):
    return g * jax.nn.sigmoid(g)


def _rmsnorm_rows(x_ref, g_ref, h_ref, rows):
    def body(r, c):
        sl = pl.ds(pl.multiple_of(r * rows, rows), rows)
        x = x_ref[sl, :]
        ms = jnp.mean(x * x, axis=-1, keepdims=True)
        h_ref[sl, :] = (x * lax.rsqrt(ms + EPS) * g_ref[...]).astype(h_ref.dtype)
        return c

    lax.fori_loop(0, x_ref.shape[0] // rows, body, 0)


def _norm_proj_kernel(x_ref, g_ref, w_ref, o_ref, h_ref, *, rows):
    @pl.when(pl.program_id(1) == 0)
    def _():
        _rmsnorm_rows(x_ref, g_ref, h_ref, rows)

    o_ref[...] = jnp.dot(h_ref[...], w_ref[...], preferred_element_type=F32).astype(o_ref.dtype)


def norm_proj(x2, gain, w, *, tm, tn):
    t, d = x2.shape
    n = w.shape[1]
    tm, tn = min(tm, t), min(tn, n)
    return pl.pallas_call(
        functools.partial(_norm_proj_kernel, rows=128),
        grid=(t // tm, n // tn),
        in_specs=[
            pl.BlockSpec((tm, d), lambda i, j: (i, 0)),
            pl.BlockSpec((1, d), lambda i, j: (0, 0)),
            pl.BlockSpec((d, tn), lambda i, j: (0, j)),
        ],
        out_specs=pl.BlockSpec((tm, tn), lambda i, j: (i, j)),
        out_shape=jax.ShapeDtypeStruct((t, n), BF16),
        scratch_shapes=[pltpu.VMEM((tm, d), BF16)],
        compiler_params=_params("parallel", "arbitrary"),
        name="norm_proj",
    )(x2, gain.reshape(1, d), w)


def _out_proj_kernel(y_ref, w_ref, x_ref, o_ref):
    o_ref[...] = x_ref[...] + jnp.dot(y_ref[...], w_ref[...], preferred_element_type=F32)


def out_proj(y2, w, x2, *, tm, tn):
    t, k = y2.shape
    d = w.shape[1]
    tm, tn = min(tm, t), min(tn, d)
    return pl.pallas_call(
        _out_proj_kernel,
        grid=(d // tn, t // tm),
        in_specs=[
            pl.BlockSpec((tm, k), lambda j, i: (i, 0)),
            pl.BlockSpec((k, tn), lambda j, i: (0, j)),
            pl.BlockSpec((tm, tn), lambda j, i: (i, j)),
        ],
        out_specs=pl.BlockSpec((tm, tn), lambda j, i: (i, j)),
        out_shape=jax.ShapeDtypeStruct((t, d), F32),
        compiler_params=_params("parallel", "parallel"),
        name="out_proj",
    )(y2, w, x2)


def _conv_layer_kernel(x_ref, gain_ref, wb_ref, wc_ref, wu_ref, wg_ref, cw_ref, cb_ref, wo_ref, o_ref,
                       h_ref, carry_ref, *, tiles_per_seq):
    i, j = pl.program_id(0), pl.program_id(1)

    @pl.when(j == 0)
    def _():
        _rmsnorm_rows(x_ref, gain_ref, h_ref, 128)

    h = h_ref[...]
    v = (jnp.dot(h, wc_ref[...], preferred_element_type=F32)
         * jnp.dot(h, wu_ref[...], preferred_element_type=F32))
    tm = v.shape[0]
    prev = jnp.where(i % tiles_per_seq == 0, 0.0, carry_ref[j])
    carry_ref[j] = v[tm - 8:, :]
    p1, p2 = prev[7:8, :], prev[6:7, :]
    row = lax.broadcasted_iota(jnp.int32, v.shape, 0)
    v1 = jnp.where(row >= 1, pltpu.roll(v, 1, 0), p1)
    v2 = jnp.where(row >= 2, pltpu.roll(v, 2, 0), jnp.where(row == 1, p1, p2))
    conv = cb_ref[...] + v2 * cw_ref[0:1, :]
    conv = conv + v1 * cw_ref[1:2, :]
    conv = conv + v * cw_ref[2:3, :]
    y = jnp.dot(h, wb_ref[...], preferred_element_type=F32) * conv
    y = y * _silu(jnp.dot(h, wg_ref[...], preferred_element_type=F32))
    contrib = jnp.dot(y.astype(BF16), wo_ref[...], preferred_element_type=F32)

    @pl.when(j == 0)
    def _():
        o_ref[...] = x_ref[...] + contrib

    @pl.when(j > 0)
    def _():
        o_ref[...] += contrib


def conv_layer(x2, s, gain, w_in, conv_w, conv_b, w_out, *, tm, ec):
    t, d = x2.shape
    e = w_out.shape[0]
    tm, ec = min(tm, s), min(ec, e)
    nj = e // ec

    def stream(g):
        return pl.BlockSpec((d, ec), lambda i, j: (0, g * nj + j))

    return pl.pallas_call(
        functools.partial(_conv_layer_kernel, tiles_per_seq=s // tm),
        grid=(t // tm, nj),
        in_specs=[pl.BlockSpec((tm, d), lambda i, j: (i, 0)),
                  pl.BlockSpec((1, d), lambda i, j: (0, 0)),
                  stream(0), stream(1), stream(2), stream(3),
                  pl.BlockSpec((CONV_K, ec), lambda i, j: (0, j)),
                  pl.BlockSpec((1, ec), lambda i, j: (0, j)),
                  pl.BlockSpec((ec, d), lambda i, j: (j, 0))],
        out_specs=pl.BlockSpec((tm, d), lambda i, j: (i, 0)),
        out_shape=jax.ShapeDtypeStruct((t, d), F32),
        scratch_shapes=[pltpu.VMEM((tm, d), BF16), pltpu.VMEM((nj, 8, ec), F32)],
        compiler_params=_params("arbitrary", "arbitrary"),
        name="conv_layer",
    )(x2, gain.reshape(1, d), w_in, w_in, w_in, w_in, conv_w, conv_b.reshape(1, e), w_out)


def _ret_core_kernel(q_ref, k_ref, v_ref, g_ref, qg_ref, kg_ref, cos_ref, sin_ref,
                     din_ref, dq_ref, dk_ref, gw_ref, gb_ref, y_ref, state_ref):
    c = RET_CHUNK
    half = RET_QK_DIM // 2

    @pl.when(pl.program_id(2) == 0)
    def _():
        state_ref[...] = jnp.zeros_like(state_ref)

    cos, sin = cos_ref[...], sin_ref[...]

    def norm_rot(t_ref, gain):
        t = t_ref[0].astype(F32)
        t = t * lax.rsqrt(jnp.mean(t * t, axis=-1, keepdims=True) + EPS) * gain
        t1, t2 = t[:, :half], t[:, half:]
        return jnp.concatenate([t1 * cos - t2 * sin, t2 * cos + t1 * sin], axis=-1)

    q = norm_rot(q_ref, qg_ref[...])
    k = norm_rot(k_ref, kg_ref[...]) * (RET_QK_DIM ** -0.5)

    din = din_ref[0]
    dq = dq_ref[0]
    dk = dk_ref[0]
    dchunk = dq[c - 1:c, :]

    for ci in range(q.shape[0] // c):
        sl = slice(ci * c, (ci + 1) * c)
        qi = q[sl].astype(BF16)
        ki = k[sl]
        vi = v_ref[0, sl, :]
        st = state_ref[...]
        inner = lax.dot_general(qi, ki.astype(BF16), (((1,), (1,)), ((), ())),
                                preferred_element_type=F32) * din
        o = jnp.dot(inner.astype(BF16), vi, preferred_element_type=F32)
        o = o + jnp.dot(qi, st.astype(BF16), preferred_element_type=F32) * dq
        kd = (ki * dk).astype(BF16)
        state_ref[...] = st * dchunk + lax.dot_general(
            kd, vi, (((0,), (0,)), ((), ())), preferred_element_type=F32)
        mu = jnp.mean(o, axis=-1, keepdims=True)
        var = jnp.mean(jnp.square(o - mu), axis=-1, keepdims=True)
        on = (o - mu) * lax.rsqrt(var + GN_EPS)
        on = on * gw_ref[...] + gb_ref[...]
        y = _silu(g_ref[0, sl, :].astype(F32)) * on
        y_ref[0, sl, :] = y.astype(y_ref.dtype)


def _ret_tables(s, heads):
    c = RET_CHUNK
    half = RET_QK_DIM // 2
    inv_freq = ROPE_BASE ** (-jnp.arange(half, dtype=F32) / half)
    ang = jnp.arange(s).astype(F32)[:, None] * inv_freq[None, :]
    log_g = jnp.log(1.0 - jnp.exp2(-5.0 - jnp.arange(heads, dtype=F32)))
    idx = jnp.arange(c, dtype=F32)
    diff = idx[:, None] - idx[None, :]
    din = jnp.where(diff >= 0, jnp.exp(log_g[:, None, None] * jnp.maximum(diff, 0.0)), 0.0)
    dq = jnp.exp(log_g[:, None] * (idx + 1.0))[..., None]
    dk = jnp.exp(log_g[:, None] * (c - 1.0 - idx))[..., None]
    return jnp.cos(ang), jnp.sin(ang), din, dq, dk


def ret_core(p3, q_gain, k_gain, gn_w, gn_b, *, ts):
    bsz, s, n = p3.shape
    heads = n // (2 * RET_QK_DIM + 2 * RET_V_DIM)
    ts = min(ts, s)
    c = RET_CHUNK
    dqk, dv = RET_QK_DIM, RET_V_DIM
    cos, sin, din, dq, dk = _ret_tables(s, heads)
    v_off = 2 * heads * dqk // dv
    g_off = v_off + heads
    return pl.pallas_call(
        _ret_core_kernel,
        grid=(bsz, heads, s // ts),
        in_specs=[
            pl.BlockSpec((1, ts, dqk), lambda b, h, t: (b, t, h)),
            pl.BlockSpec((1, ts, dqk), lambda b, h, t: (b, t, heads + h)),
            pl.BlockSpec((1, ts, dv), lambda b, h, t: (b, t, v_off + h)),
            pl.BlockSpec((1, ts, dv), lambda b, h, t: (b, t, g_off + h)),
            pl.BlockSpec((1, dqk), lambda b, h, t: (0, 0)),
            pl.BlockSpec((1, dqk), lambda b, h, t: (0, 0)),
            pl.BlockSpec((ts, dqk // 2), lambda b, h, t: (t, 0)),
            pl.BlockSpec((ts, dqk // 2), lambda b, h, t: (t, 0)),
            pl.BlockSpec((1, c, c), lambda b, h, t: (h, 0, 0)),
            pl.BlockSpec((1, c, 1), lambda b, h, t: (h, 0, 0)),
            pl.BlockSpec((1, c, 1), lambda b, h, t: (h, 0, 0)),
            pl.BlockSpec((1, dv), lambda b, h, t: (0, h)),
            pl.BlockSpec((1, dv), lambda b, h, t: (0, h)),
        ],
        out_specs=pl.BlockSpec((1, ts, dv), lambda b, h, t: (b, t, h)),
        out_shape=jax.ShapeDtypeStruct((bsz, s, heads * dv), BF16),
        scratch_shapes=[pltpu.VMEM((dqk, dv), F32)],
        compiler_params=_params("parallel", "parallel", "arbitrary"),
        name="ret_core",
    )(p3, p3, p3, p3, q_gain.reshape(1, dqk), k_gain.reshape(1, dqk), cos, sin, din, dq, dk,
      gn_w.reshape(1, heads * dv), gn_b.reshape(1, heads * dv))


def _sb_core_kernel(q_ref, k_ref, v_ref, g_ref, qg_ref, kg_ref, u_ref, y_ref,
                    qs_ref, ks_ref, acc_ref, run_ref, *, tb, hp):
    d = SB_HEAD_DIM
    s = q_ref.shape[1]
    nt = s // tb
    row = lax.broadcasted_iota(jnp.int32, (tb, tb), 0)
    col = lax.broadcasted_iota(jnp.int32, (tb, tb), 1)
    causal = col < row

    def rms(t, gain):
        return t * lax.rsqrt(jnp.mean(t * t, axis=-1, keepdims=True) + EPS) * gain

    def scores(qi, hh, j0):
        kj = ks_ref[hh, pl.ds(j0, tb), :]
        return lax.dot_general(qi, kj, (((1,), (1,)), ((), ())), preferred_element_type=F32)

    def softplus(z):
        return jnp.maximum(z, 0.0) + jnp.log(1.0 + jnp.exp(-jnp.abs(z)))

    def suffix(sp):
        hi = sp.astype(BF16)
        lo = (sp - hi.astype(F32)).astype(BF16)
        return (jnp.dot(hi, u_ref[...], preferred_element_type=F32)
                + jnp.dot(lo, u_ref[...], preferred_element_type=F32))

    def values(hh, j0):
        return v_ref[0, pl.ds(j0, tb), hh * d:(hh + 1) * d]

    def diag_tile(qi, hh, i0):
        z = scores(qi, hh, i0)
        sp = jnp.where(causal, softplus(z), 0.0)
        suf = suffix(sp)
        w = jnp.where(causal, jnp.exp(z - sp - suf), 0.0)
        acc = jnp.dot(w.astype(BF16), values(hh, i0), preferred_element_type=F32)
        return acc, suf[:, 0:1] + sp[:, 0:1]

    def off_tile(qi, hh, j0, acc, run):
        z = scores(qi, hh, j0)
        sp = softplus(z)
        suf = suffix(sp)
        w = jnp.exp(z - sp - (suf + run))
        acc = acc + jnp.dot(w.astype(BF16), values(hh, j0), preferred_element_type=F32)
        return acc, run + suf[:, 0:1] + sp[:, 0:1]

    def finish(hh, i0, acc):
        g = g_ref[0, pl.ds(i0, tb), hh * d:(hh + 1) * d].astype(F32)
        y_ref[0, pl.ds(i0, tb), hh * d:(hh + 1) * d] = (_silu(g) * acc).astype(y_ref.dtype)

    for hh in range(hp):
        cs = slice(hh * d, (hh + 1) * d)
        qs_ref[hh] = (rms(q_ref[0, :, cs].astype(F32), qg_ref[...]) * (d ** -0.5)).astype(BF16)
        ks_ref[hh] = rms(k_ref[0, :, cs].astype(F32), kg_ref[...]).astype(BF16)

    for hh in range(hp):
        acc, _ = diag_tile(qs_ref[hh, 0:tb, :], hh, 0)
        finish(hh, 0, acc)

    def qtile(i, carry):
        i0 = pl.multiple_of(i * tb, tb)
        for hh in range(hp):
            qi = qs_ref[hh, pl.ds(i0, tb), :]
            acc, run = diag_tile(qi, hh, i0)
            acc, run = off_tile(qi, hh, pl.multiple_of(i0 - tb, tb), acc, run)
            acc_ref[hh] = acc
            run_ref[hh] = run

            def more(c):
                j, rmin = c
                return jnp.logical_and(j >= 0, rmin <= SB_UNDERFLOW)

            def ktile(c, qi=qi, hh=hh):
                j, _ = c
                a, r = off_tile(qi, hh, pl.multiple_of(j * tb, tb), acc_ref[hh], run_ref[hh])
                acc_ref[hh] = a
                run_ref[hh] = r
                return j - 1, jnp.min(r)

            lax.while_loop(more, ktile, (i - 2, jnp.min(run)))
            finish(hh, i0, acc_ref[hh])
        return carry

    lax.fori_loop(1, nt, qtile, 0)


def sb_core(p3, q_gain, k_gain, *, tb, hp):
    bsz, s, n4 = p3.shape
    d = SB_HEAD_DIM
    width = n4 // 4
    heads = width // d
    hp = min(hp, heads)
    tb = min(tb, s)
    ng = heads // hp
    u = (jnp.arange(tb)[:, None] > jnp.arange(tb)[None, :]).astype(BF16)

    def stream(g):
        return pl.BlockSpec((1, s, hp * d), lambda b, h: (b, 0, g * ng + h))

    return pl.pallas_call(
        functools.partial(_sb_core_kernel, tb=tb, hp=hp),
        grid=(bsz, ng),
        in_specs=[stream(0), stream(1), stream(2), stream(3),
                  pl.BlockSpec((1, d), lambda b, h: (0, 0)),
                  pl.BlockSpec((1, d), lambda b, h: (0, 0)),
                  pl.BlockSpec((tb, tb), lambda b, h: (0, 0))],
        out_specs=pl.BlockSpec((1, s, hp * d), lambda b, h: (b, 0, h)),
        out_shape=jax.ShapeDtypeStruct((bsz, s, width), BF16),
        scratch_shapes=[pltpu.VMEM((hp, s, d), BF16), pltpu.VMEM((hp, s, d), BF16),
                        pltpu.VMEM((hp, tb, d), F32), pltpu.VMEM((hp, tb, 1), F32)],
        compiler_params=_params("parallel", "parallel"),
        name="sb_core",
    )(p3, p3, p3, p3, q_gain.reshape(1, d), k_gain.reshape(1, d), u)


def kernel(x, conv_norm, conv_w_in, conv_w, conv_b, conv_w_out, ret_norm, ret_w_in, ret_q_gain, ret_k_gain,
           ret_gn_w, ret_gn_b, ret_w_out, sb_norm, sb_w_in, sb_q_gain, sb_k_gain, sb_w_out):
    bsz, s, d = x.shape
    depth = conv_norm.shape[0] + ret_norm.shape[0] + sb_norm.shape[0]
    x2 = x.reshape(bsz * s, d)
    for i in range(depth):
        kind, j = i % N_MIXERS, i // N_MIXERS
        if kind == 0:
            x2 = conv_layer(x2, s, conv_norm[j], conv_w_in[j].astype(BF16), conv_w[j], conv_b[j],
                            conv_w_out[j].astype(BF16), tm=512, ec=512)
            continue
        elif kind == 1:
            p = norm_proj(x2, ret_norm[j], ret_w_in[j].astype(BF16), tm=1024, tn=1024)
            y = ret_core(p.reshape(bsz, s, -1), ret_q_gain[j], ret_k_gain[j], ret_gn_w[j], ret_gn_b[j], ts=512)
            w_out = ret_w_out[j]
        else:
            p = norm_proj(x2, sb_norm[j], sb_w_in[j].astype(BF16), tm=1024, tn=1024)
            y = sb_core(p.reshape(bsz, s, -1), sb_q_gain[j], sb_k_gain[j], tb=256, hp=2)
            w_out = sb_w_out[j]
        x2 = out_proj(y.reshape(bsz * s, -1), w_out.astype(BF16), x2, tm=1024, tn=1024)
    return x2.reshape(bsz, s, d)
```

```python
import functools
import math

import jax
import jax.numpy as jnp
from jax import lax
from jax.experimental import pallas as pl
from jax.experimental.pallas import tpu as pltpu

EPS = 1e-6
GN_EPS = 1e-5
N_MIXERS = 3
CONV_K = 3
RET_QK_DIM = 256
RET_V_DIM = 512
RET_CHUNK = 128
ROPE_BASE = 10000.0
SB_HEAD_DIM = 128
SB_UNDERFLOW = 104.0

F32 = jnp.float32
BF16 = jnp.bfloat16

VMEM_LIMIT_BYTES = 56 * 1024 * 1024


def _params(*sem):
    return pltpu.CompilerParams(dimension_semantics=sem, vmem_limit_bytes=VMEM_LIMIT_BYTES)


def _silu(g):
    return g * jax.nn.sigmoid(g)


def _rmsnorm_rows(x_ref, g_ref, h_ref, rows, copy_ref=None):
    def body(r, c):
        sl = pl.ds(pl.multiple_of(r * rows, rows), rows)
        x = x_ref[sl, :]
        ms = jnp.mean(x * x, axis=-1, keepdims=True)
        h_ref[sl, :] = (x * lax.rsqrt(ms + EPS) * g_ref[...]).astype(h_ref.dtype)
        if copy_ref is not None:
            copy_ref[sl, :] = x
        return c

    lax.fori_loop(0, x_ref.shape[0] // rows, body, 0)


def _norm_proj_kernel(x_ref, g_ref, w_ref, o_ref, h_ref, *, rows):
    @pl.when(pl.program_id(1) == 0)
    def _():
        _rmsnorm_rows(x_ref, g_ref, h_ref, rows)

    o_ref[...] = jnp.dot(h_ref[...], w_ref[...], preferred_element_type=F32).astype(o_ref.dtype)


def norm_proj(x2, gain, w, *, tm, tn):
    t, d = x2.shape
    n = w.shape[1]
    tm, tn = min(tm, t), min(tn, n)
    return pl.pallas_call(
        functools.partial(_norm_proj_kernel, rows=128),
        grid=(t // tm, n // tn),
        in_specs=[
            pl.BlockSpec((tm, d), lambda i, j: (i, 0)),
            pl.BlockSpec((1, d), lambda i, j: (0, 0)),
            pl.BlockSpec((d, tn), lambda i, j: (0, j)),
        ],
        out_specs=pl.BlockSpec((tm, tn), lambda i, j: (i, j)),
        out_shape=jax.ShapeDtypeStruct((t, n), BF16),
        scratch_shapes=[pltpu.VMEM((tm, d), BF16)],
        compiler_params=_params("parallel", "arbitrary"),
        name="norm_proj",
    )(x2, gain.reshape(1, d), w)


def _out_proj_kernel(y_ref, w_ref, x_ref, o_ref):
    o_ref[...] = x_ref[...] + jnp.dot(y_ref[...], w_ref[...], preferred_element_type=F32)


def out_proj(y2, w, x2, *, tm, tn):
    t, k = y2.shape
    d = w.shape[1]
    tm, tn = min(tm, t), min(tn, d)
    return pl.pallas_call(
        _out_proj_kernel,
        grid=(d // tn, t // tm),
        in_specs=[
            pl.BlockSpec((tm, k), lambda j, i: (i, 0)),
            pl.BlockSpec((k, tn), lambda j, i: (0, j)),
            pl.BlockSpec((tm, tn), lambda j, i: (i, j)),
        ],
        out_specs=pl.BlockSpec((tm, tn), lambda j, i: (i, j)),
        out_shape=jax.ShapeDtypeStruct((t, d), F32),
        compiler_params=_params("parallel", "parallel"),
        name="out_proj",
    )(y2, w, x2)


def _conv_layer_kernel(x_ref, gain_ref, wb_ref, wc_ref, wu_ref, wg_ref, cw_ref, cb_ref, wo_ref, o_ref,
                       h_ref, carry_ref, *, tiles_per_seq):
    i, j = pl.program_id(0), pl.program_id(1)

    @pl.when(j == 0)
    def _():
        _rmsnorm_rows(x_ref, gain_ref, h_ref, 128, copy_ref=o_ref)

    h = h_ref[...]
    v = (jnp.dot(h, wc_ref[...], preferred_element_type=F32)
         * jnp.dot(h, wu_ref[...], preferred_element_type=F32))
    tm = v.shape[0]
    prev = jnp.where(i % tiles_per_seq == 0, 0.0, carry_ref[j])
    carry_ref[j] = v[tm - 8:, :]
    p1, p2 = prev[7:8, :], prev[6:7, :]
    row = lax.broadcasted_iota(jnp.int32, v.shape, 0)
    v1 = jnp.where(row >= 1, pltpu.roll(v, 1, 0), p1)
    v2 = jnp.where(row >= 2, pltpu.roll(v, 2, 0), jnp.where(row == 1, p1, p2))
    conv = cb_ref[...] + v2 * cw_ref[0:1, :]
    conv = conv + v1 * cw_ref[1:2, :]
    conv = conv + v * cw_ref[2:3, :]
    y = jnp.dot(h, wb_ref[...], preferred_element_type=F32) * conv
    y = y * _silu(jnp.dot(h, wg_ref[...], preferred_element_type=F32))
    o_ref[...] += jnp.dot(y.astype(BF16), wo_ref[...], preferred_element_type=F32)


def conv_layer(x2, s, gain, w_in, conv_w, conv_b, w_out, *, tm, ec):
    t, d = x2.shape
    e = w_out.shape[0]
    tm, ec = min(tm, s), min(ec, e)
    nj = e // ec

    def stream(g):
        return pl.BlockSpec((d, ec), lambda i, j: (0, g * nj + j))

    return pl.pallas_call(
        functools.partial(_conv_layer_kernel, tiles_per_seq=s // tm),
        grid=(t // tm, nj),
        in_specs=[pl.BlockSpec((tm, d), lambda i, j: (i, 0)),
                  pl.BlockSpec((1, d), lambda i, j: (0, 0)),
                  stream(0), stream(1), stream(2), stream(3),
                  pl.BlockSpec((CONV_K, ec), lambda i, j: (0, j)),
                  pl.BlockSpec((1, ec), lambda i, j: (0, j)),
                  pl.BlockSpec((ec, d), lambda i, j: (j, 0))],
        out_specs=pl.BlockSpec((tm, d), lambda i, j: (i, 0)),
        out_shape=jax.ShapeDtypeStruct((t, d), F32),
        scratch_shapes=[pltpu.VMEM((tm, d), BF16), pltpu.VMEM((nj, 8, ec), F32)],
        compiler_params=_params("arbitrary", "arbitrary"),
        name="conv_layer",
    )(x2, gain.reshape(1, d), w_in, w_in, w_in, w_in, conv_w, conv_b.reshape(1, e), w_out)


def _ret_core_kernel(q_ref, k_ref, v_ref, g_ref, qg_ref, kg_ref, cos_ref, sin_ref,
                     din_ref, dq_ref, dk_ref, gw_ref, gb_ref, y_ref, state_ref):
    c = RET_CHUNK
    half = RET_QK_DIM // 2

    @pl.when(pl.program_id(2) == 0)
    def _():
        state_ref[...] = jnp.zeros_like(state_ref)

    cos, sin = cos_ref[...], sin_ref[...]

    def norm_rot(t_ref, gain):
        t = t_ref[0].astype(F32)
        t = t * lax.rsqrt(jnp.mean(t * t, axis=-1, keepdims=True) + EPS) * gain
        t1, t2 = t[:, :half], t[:, half:]
        return jnp.concatenate([t1 * cos - t2 * sin, t2 * cos + t1 * sin], axis=-1)

    q = norm_rot(q_ref, qg_ref[...])
    k = norm_rot(k_ref, kg_ref[...]) * (RET_QK_DIM ** -0.5)

    din = din_ref[0]
    dq = dq_ref[0]
    dk = dk_ref[0]
    dchunk = dq[c - 1:c, :]

    for ci in range(q.shape[0] // c):
        sl = slice(ci * c, (ci + 1) * c)
        qi = q[sl].astype(BF16)
        ki = k[sl]
        vi = v_ref[0, sl, :]
        st = state_ref[...]
        inner = lax.dot_general(qi, ki.astype(BF16), (((1,), (1,)), ((), ())),
                                preferred_element_type=F32) * din
        o = jnp.dot(inner.astype(BF16), vi, preferred_element_type=F32)
        o = o + jnp.dot(qi, st.astype(BF16), preferred_element_type=F32) * dq
        kd = (ki * dk).astype(BF16)
        state_ref[...] = st * dchunk + lax.dot_general(
            kd, vi, (((0,), (0,)), ((), ())), preferred_element_type=F32)
        mu = jnp.mean(o, axis=-1, keepdims=True)
        var = jnp.mean(jnp.square(o - mu), axis=-1, keepdims=True)
        on = (o - mu) * lax.rsqrt(var + GN_EPS)
        on = on * gw_ref[...] + gb_ref[...]
        y = _silu(g_ref[0, sl, :].astype(F32)) * on
        y_ref[0, sl, :] = y.astype(y_ref.dtype)


def _ret_tables(s, heads):
    c = RET_CHUNK
    half = RET_QK_DIM // 2
    inv_freq = ROPE_BASE ** (-jnp.arange(half, dtype=F32) / half)
    ang = jnp.arange(s).astype(F32)[:, None] * inv_freq[None, :]
    log_g = jnp.log(1.0 - jnp.exp2(-5.0 - jnp.arange(heads, dtype=F32)))
    idx = jnp.arange(c, dtype=F32)
    diff = idx[:, None] - idx[None, :]
    din = jnp.where(diff >= 0, jnp.exp(log_g[:, None, None] * jnp.maximum(diff, 0.0)), 0.0)
    dq = jnp.exp(log_g[:, None] * (idx + 1.0))[..., None]
    dk = jnp.exp(log_g[:, None] * (c - 1.0 - idx))[..., None]
    return jnp.cos(ang), jnp.sin(ang), din, dq, dk


def ret_core(p3, q_gain, k_gain, gn_w, gn_b, *, ts):
    bsz, s, n = p3.shape
    heads = n // (2 * RET_QK_DIM + 2 * RET_V_DIM)
    ts = min(ts, s)
    c = RET_CHUNK
    dqk, dv = RET_QK_DIM, RET_V_DIM
    cos, sin, din, dq, dk = _ret_tables(s, heads)
    v_off = 2 * heads * dqk // dv
    g_off = v_off + heads
    return pl.pallas_call(
        _ret_core_kernel,
        grid=(bsz, heads, s // ts),
        in_specs=[
            pl.BlockSpec((1, ts, dqk), lambda b, h, t: (b, t, h)),
            pl.BlockSpec((1, ts, dqk), lambda b, h, t: (b, t, heads + h)),
            pl.BlockSpec((1, ts, dv), lambda b, h, t: (b, t, v_off + h)),
            pl.BlockSpec((1, ts, dv), lambda b, h, t: (b, t, g_off + h)),
            pl.BlockSpec((1, dqk), lambda b, h, t: (0, 0)),
            pl.BlockSpec((1, dqk), lambda b, h, t: (0, 0)),
            pl.BlockSpec((ts, dqk // 2), lambda b, h, t: (t, 0)),
            pl.BlockSpec((ts, dqk // 2), lambda b, h, t: (t, 0)),
            pl.BlockSpec((1, c, c), lambda b, h, t: (h, 0, 0)),
            pl.BlockSpec((1, c, 1), lambda b, h, t: (h, 0, 0)),
            pl.BlockSpec((1, c, 1), lambda b, h, t: (h, 0, 0)),
            pl.BlockSpec((1, dv), lambda b, h, t: (0, h)),
            pl.BlockSpec((1, dv), lambda b, h, t: (0, h)),
        ],
        out_specs=pl.BlockSpec((1, ts, dv), lambda b, h, t: (b, t, h)),
        out_shape=jax.ShapeDtypeStruct((bsz, s, heads * dv), BF16),
        scratch_shapes=[pltpu.VMEM((dqk, dv), F32)],
        compiler_params=_params("parallel", "parallel", "arbitrary"),
        name="ret_core",
    )(p3, p3, p3, p3, q_gain.reshape(1, dqk), k_gain.reshape(1, dqk), cos, sin, din, dq, dk,
      gn_w.reshape(1, heads * dv), gn_b.reshape(1, heads * dv))


def _sb_core_kernel(q_ref, k_ref, v_ref, g_ref, qg_ref, kg_ref, u_ref, y_ref,
                    qs_ref, ks_ref, acc_ref, run_ref, *, tb, hp):
    d = SB_HEAD_DIM
    s = q_ref.shape[1]
    nt = s // tb
    row = lax.broadcasted_iota(jnp.int32, (tb, tb), 0)
    col = lax.broadcasted_iota(jnp.int32, (tb, tb), 1)
    causal = col < row

    def rms(t, gain):
        return t * lax.rsqrt(jnp.mean(t * t, axis=-1, keepdims=True) + EPS) * gain

    def scores(qi, hh, j0):
        kj = ks_ref[hh, pl.ds(j0, tb), :]
        return lax.dot_general(qi, kj, (((1,), (1,)), ((), ())), preferred_element_type=F32)

    def softplus(z):
        return jnp.maximum(z, 0.0) + jnp.log(1.0 + jnp.exp(-jnp.abs(z)))

    def suffix(sp):
        hi = sp.astype(BF16)
        lo = (sp - hi.astype(F32)).astype(BF16)
        return jnp.dot(jnp.concatenate([hi, lo], axis=1), u_ref[...], preferred_element_type=F32)

    def values(hh, j0):
        return v_ref[0, pl.ds(j0, tb), hh * d:(hh + 1) * d]

    def diag_tile(qi, hh, i0):
        z = scores(qi, hh, i0)
        sp = jnp.where(causal, softplus(z), 0.0)
        suf = suffix(sp)
        w = jnp.where(causal, jnp.exp(z - sp - suf), 0.0)
        acc = jnp.dot(w.astype(BF16), values(hh, i0), preferred_element_type=F32)
        return acc, suf[:, 0:1] + sp[:, 0:1]

    def off_tile(qi, hh, j0, acc, run):
        z = scores(qi, hh, j0)
        sp = softplus(z)
        suf = suffix(sp)
        w = jnp.exp(z - sp - (suf + run))
        acc = acc + jnp.dot(w.astype(BF16), values(hh, j0), preferred_element_type=F32)
        return acc, run + suf[:, 0:1] + sp[:, 0:1]

    def finish(hh, i0, acc):
        g = g_ref[0, pl.ds(i0, tb), hh * d:(hh + 1) * d].astype(F32)
        y_ref[0, pl.ds(i0, tb), hh * d:(hh + 1) * d] = (_silu(g) * acc).astype(y_ref.dtype)

    for hh in range(hp):
        cs = slice(hh * d, (hh + 1) * d)
        qs_ref[hh] = (rms(q_ref[0, :, cs].astype(F32), qg_ref[...]) * (d ** -0.5)).astype(BF16)
        ks_ref[hh] = rms(k_ref[0, :, cs].astype(F32), kg_ref[...]).astype(BF16)

    for hh in range(hp):
        acc, _ = diag_tile(qs_ref[hh, 0:tb, :], hh, 0)
        finish(hh, 0, acc)

    def qtile(i, carry):
        i0 = pl.multiple_of(i * tb, tb)
        rmin = []
        for hh in range(hp):
            qi = qs_ref[hh, pl.ds(i0, tb), :]
            acc, run = diag_tile(qi, hh, i0)
            acc, run = off_tile(qi, hh, pl.multiple_of(i0 - tb, tb), acc, run)
            acc_ref[hh] = acc
            run_ref[hh] = run
            rmin.append(jnp.min(run))

        for hh in range(hp):
            def more(c):
                j, r = c
                return jnp.logical_and(j >= 0, r <= SB_UNDERFLOW)

            def ktile(c, hh=hh):
                j, _ = c
                qi = qs_ref[hh, pl.ds(i0, tb), :]
                a, r = off_tile(qi, hh, pl.multiple_of(j * tb, tb), acc_ref[hh], run_ref[hh])
                acc_ref[hh] = a
                run_ref[hh] = r
                return j - 1, jnp.min(r)

            lax.while_loop(more, ktile, (i - 2, rmin[hh]))

        for hh in range(hp):
            finish(hh, i0, acc_ref[hh])
        return carry

    lax.fori_loop(1, nt, qtile, 0)


def sb_core(p3, q_gain, k_gain, *, tb, hp):
    bsz, s, n4 = p3.shape
    d = SB_HEAD_DIM
    width = n4 // 4
    heads = width // d
    hp = min(hp, heads)
    tb = min(tb, s)
    ng = heads // hp
    u = (jnp.arange(tb)[:, None] > jnp.arange(tb)[None, :]).astype(BF16)
    u = jnp.concatenate([u, u], axis=0)

    def stream(g):
        return pl.BlockSpec((1, s, hp * d), lambda b, h: (b, 0, g * ng + h))

    return pl.pallas_call(
        functools.partial(_sb_core_kernel, tb=tb, hp=hp),
        grid=(bsz, ng),
        in_specs=[stream(0), stream(1), stream(2), stream(3),
                  pl.BlockSpec((1, d), lambda b, h: (0, 0)),
                  pl.BlockSpec((1, d), lambda b, h: (0, 0)),
                  pl.BlockSpec((2 * tb, tb), lambda b, h: (0, 0))],
        out_specs=pl.BlockSpec((1, s, hp * d), lambda b, h: (b, 0, h)),
        out_shape=jax.ShapeDtypeStruct((bsz, s, width), BF16),
        scratch_shapes=[pltpu.VMEM((hp, s, d), BF16), pltpu.VMEM((hp, s, d), BF16),
                        pltpu.VMEM((hp, tb, d), F32), pltpu.VMEM((hp, tb, 1), F32)],
        compiler_params=_params("parallel", "parallel"),
        name="sb_core",
    )(p3, p3, p3, p3, q_gain.reshape(1, d), k_gain.reshape(1, d), u)


def kernel(x, conv_norm, conv_w_in, conv_w, conv_b, conv_w_out, ret_norm, ret_w_in, ret_q_gain, ret_k_gain,
           ret_gn_w, ret_gn_b, ret_w_out, sb_norm, sb_w_in, sb_q_gain, sb_k_gain, sb_w_out):
    bsz, s, d = x.shape
    depth = conv_norm.shape[0] + ret_norm.shape[0] + sb_norm.shape[0]
    x2 = x.reshape(bsz * s, d)
    for i in range(depth):
        kind, j = i % N_MIXERS, i // N_MIXERS
        if kind == 0:
            x2 = conv_layer(x2, s, conv_norm[j], conv_w_in[j].astype(BF16), conv_w[j], conv_b[j],
                            conv_w_out[j].astype(BF16), tm=512, ec=512)
            continue
        elif kind == 1:
            p = norm_proj(x2, ret_norm[j], ret_w_in[j].astype(BF16), tm=1024, tn=1024)
            y = ret_core(p.reshape(bsz, s, -1), ret_q_gain[j], ret_k_gain[j], ret_gn_w[j], ret_gn_b[j], ts=512)
            w_out = ret_w_out[j]
        else:
            p = norm_proj(x2, sb_norm[j], sb_w_in[j].astype(BF16), tm=1024, tn=1024)
            y = sb_core(p.reshape(bsz, s, -1), sb_q_gain[j], sb_k_gain[j], tb=256, hp=2)
            w_out = sb_w_out[j]
        x2 = out_proj(y.reshape(bsz * s, -1), w_out.astype(BF16), x2, tm=1024, tn=1024)
    return x2.reshape(bsz, s, d)
```

```python
import functools
import math

import jax
import jax.numpy as jnp
from jax import lax
from jax.experimental import pallas as pl
from jax.experimental.pallas import tpu as pltpu

EPS = 1e-6
GN_EPS = 1e-5
N_MIXERS = 3
CONV_K = 3
RET_QK_DIM = 256
RET_V_DIM = 512
RET_CHUNK = 128
ROPE_BASE = 10000.0
SB_HEAD_DIM = 128
SB_UNDERFLOW = 104.0

F32 = jnp.float32
BF16 = jnp.bfloat16

VMEM_LIMIT_BYTES = 56 * 1024 * 1024


def _params(*sem):
    return pltpu.CompilerParams(dimension_semantics=sem, vmem_limit_bytes=VMEM_LIMIT_BYTES)


def _silu(g):
    return g * jax.nn.sigmoid(g)


def _rmsnorm_rows(x_ref, g_ref, h_ref, rows, copy_ref=None):
    def body(r, c):
        sl = pl.ds(pl.multiple_of(r * rows, rows), rows)
        x = x_ref[sl, :]
        ms = jnp.mean(x * x, axis=-1, keepdims=True)
        h_ref[sl, :] = (x * lax.rsqrt(ms + EPS) * g_ref[...]).astype(h_ref.dtype)
        if copy_ref is not None:
            copy_ref[sl, :] = x
        return c

    lax.fori_loop(0, x_ref.shape[0] // rows, body, 0)


def _norm_proj_kernel(x_ref, g_ref, w_ref, o_ref, h_ref, *, rows):
    @pl.when(pl.program_id(1) == 0)
    def _():
        _rmsnorm_rows(x_ref, g_ref, h_ref, rows)

    o_ref[...] = jnp.dot(h_ref[...], w_ref[...], preferred_element_type=F32).astype(o_ref.dtype)


def norm_proj(x2, gain, w, *, tm, tn):
    t, d = x2.shape
    n = w.shape[1]
    tm, tn = min(tm, t), min(tn, n)
    return pl.pallas_call(
        functools.partial(_norm_proj_kernel, rows=128),
        grid=(t // tm, n // tn),
        in_specs=[
            pl.BlockSpec((tm, d), lambda i, j: (i, 0)),
            pl.BlockSpec((1, d), lambda i, j: (0, 0)),
            pl.BlockSpec((d, tn), lambda i, j: (0, j)),
        ],
        out_specs=pl.BlockSpec((tm, tn), lambda i, j: (i, j)),
        out_shape=jax.ShapeDtypeStruct((t, n), BF16),
        scratch_shapes=[pltpu.VMEM((tm, d), BF16)],
        compiler_params=_params("parallel", "arbitrary"),
        name="norm_proj",
    )(x2, gain.reshape(1, d), w)


def _out_proj_kernel(y_ref, w_ref, x_ref, o_ref):
    o_ref[...] = x_ref[...] + jnp.dot(y_ref[...], w_ref[...], preferred_element_type=F32)


def out_proj(y2, w, x2, *, tm, tn):
    t, k = y2.shape
    d = w.shape[1]
    tm, tn = min(tm, t), min(tn, d)
    return pl.pallas_call(
        _out_proj_kernel,
        grid=(d // tn, t // tm),
        in_specs=[
            pl.BlockSpec((tm, k), lambda j, i: (i, 0)),
            pl.BlockSpec((k, tn), lambda j, i: (0, j)),
            pl.BlockSpec((tm, tn), lambda j, i: (i, j)),
        ],
        out_specs=pl.BlockSpec((tm, tn), lambda j, i: (i, j)),
        out_shape=jax.ShapeDtypeStruct((t, d), F32),
        compiler_params=_params("parallel", "parallel"),
        name="out_proj",
    )(y2, w, x2)


def _conv_layer_kernel(x_ref, gain_ref, wb_ref, wc_ref, wu_ref, wg_ref, cw_ref, cb_ref, wo_ref, o_ref,
                       h_ref, carry_ref, *, tiles_per_seq):
    i, j = pl.program_id(0), pl.program_id(1)

    @pl.when(j == 0)
    def _():
        _rmsnorm_rows(x_ref, gain_ref, h_ref, 128, copy_ref=o_ref)

    h = h_ref[...]
    v = (jnp.dot(h, wc_ref[...], preferred_element_type=F32)
         * jnp.dot(h, wu_ref[...], preferred_element_type=F32))
    tm = v.shape[0]
    prev = jnp.where(i % tiles_per_seq == 0, 0.0, carry_ref[j])
    carry_ref[j] = v[tm - 8:, :]
    p1, p2 = prev[7:8, :], prev[6:7, :]
    row = lax.broadcasted_iota(jnp.int32, v.shape, 0)
    v1 = jnp.where(row >= 1, pltpu.roll(v, 1, 0), p1)
    v2 = jnp.where(row >= 2, pltpu.roll(v, 2, 0), jnp.where(row == 1, p1, p2))
    conv = cb_ref[...] + v2 * cw_ref[0:1, :]
    conv = conv + v1 * cw_ref[1:2, :]
    conv = conv + v * cw_ref[2:3, :]
    y = jnp.dot(h, wb_ref[...], preferred_element_type=F32) * conv
    y = y * _silu(jnp.dot(h, wg_ref[...], preferred_element_type=F32))
    o_ref[...] += jnp.dot(y.astype(BF16), wo_ref[...], preferred_element_type=F32)


def conv_layer(x2, s, gain, w_in, conv_w, conv_b, w_out, *, tm, ec):
    t, d = x2.shape
    e = w_out.shape[0]
    tm, ec = min(tm, s), min(ec, e)
    nj = e // ec

    def stream(g):
        return pl.BlockSpec((d, ec), lambda i, j: (0, g * nj + j))

    return pl.pallas_call(
        functools.partial(_conv_layer_kernel, tiles_per_seq=s // tm),
        grid=(t // tm, nj),
        in_specs=[pl.BlockSpec((tm, d), lambda i, j: (i, 0)),
                  pl.BlockSpec((1, d), lambda i, j: (0, 0)),
                  stream(0), stream(1), stream(2), stream(3),
                  pl.BlockSpec((CONV_K, ec), lambda i, j: (0, j)),
                  pl.BlockSpec((1, ec), lambda i, j: (0, j)),
                  pl.BlockSpec((ec, d), lambda i, j: (j, 0))],
        out_specs=pl.BlockSpec((tm, d), lambda i, j: (i, 0)),
        out_shape=jax.ShapeDtypeStruct((t, d), F32),
        scratch_shapes=[pltpu.VMEM((tm, d), BF16), pltpu.VMEM((nj, 8, ec), F32)],
        compiler_params=_params("arbitrary", "arbitrary"),
        name="conv_layer",
    )(x2, gain.reshape(1, d), w_in, w_in, w_in, w_in, conv_w, conv_b.reshape(1, e), w_out)


def _ret_layer_kernel(x_ref, gain_ref, wq_ref, wk_ref, wv_ref, wg_ref, qg_ref, kg_ref, cos_ref, sin_ref,
                      din_ref, dq_ref, dk_ref, gw_ref, gb_ref, wo_ref, o_ref, h_ref, state_ref, *, tiles_per_seq):
    i, hd = pl.program_id(0), pl.program_id(1)
    c = RET_CHUNK
    half = RET_QK_DIM // 2

    @pl.when(hd == 0)
    def _():
        _rmsnorm_rows(x_ref, gain_ref, h_ref, 128, copy_ref=o_ref)

    @pl.when(i % tiles_per_seq == 0)
    def _():
        state_ref[hd] = jnp.zeros(state_ref.shape[1:], F32)

    h = h_ref[...]
    cos, sin = cos_ref[...], sin_ref[...]

    def norm_rot(w_ref, gain):
        t = jnp.dot(h, w_ref[...], preferred_element_type=F32)
        t = t * lax.rsqrt(jnp.mean(t * t, axis=-1, keepdims=True) + EPS) * gain
        t1, t2 = t[:, :half], t[:, half:]
        return jnp.concatenate([t1 * cos - t2 * sin, t2 * cos + t1 * sin], axis=-1)

    q = norm_rot(wq_ref, qg_ref[...]).astype(BF16)
    k = norm_rot(wk_ref, kg_ref[...]) * (RET_QK_DIM ** -0.5)
    v = jnp.dot(h, wv_ref[...], preferred_element_type=F32).astype(BF16)
    g = jnp.dot(h, wg_ref[...], preferred_element_type=F32)

    din = din_ref[0]
    dq = dq_ref[0]
    dk = dk_ref[0]
    dchunk = dq[c - 1:c, :]

    ys = []
    for ci in range(q.shape[0] // c):
        sl = slice(ci * c, (ci + 1) * c)
        qi, ki, vi = q[sl], k[sl], v[sl]
        st = state_ref[hd]
        inner = lax.dot_general(qi, ki.astype(BF16), (((1,), (1,)), ((), ())),
                                preferred_element_type=F32) * din
        o = jnp.dot(inner.astype(BF16), vi, preferred_element_type=F32)
        o = o + jnp.dot(qi, st.astype(BF16), preferred_element_type=F32) * dq
        kd = (ki * dk).astype(BF16)
        state_ref[hd] = st * dchunk + lax.dot_general(
            kd, vi, (((0,), (0,)), ((), ())), preferred_element_type=F32)
        mu = jnp.mean(o, axis=-1, keepdims=True)
        var = jnp.mean(jnp.square(o - mu), axis=-1, keepdims=True)
        on = (o - mu) * lax.rsqrt(var + GN_EPS)
        on = on * gw_ref[...] + gb_ref[...]
        ys.append((_silu(g[sl]) * on).astype(BF16))
    y = jnp.concatenate(ys, axis=0)
    o_ref[...] += jnp.dot(y, wo_ref[...], preferred_element_type=F32)


def _ret_tables(s, heads):
    c = RET_CHUNK
    half = RET_QK_DIM // 2
    inv_freq = ROPE_BASE ** (-jnp.arange(half, dtype=F32) / half)
    ang = jnp.arange(s).astype(F32)[:, None] * inv_freq[None, :]
    log_g = jnp.log(1.0 - jnp.exp2(-5.0 - jnp.arange(heads, dtype=F32)))
    idx = jnp.arange(c, dtype=F32)
    diff = idx[:, None] - idx[None, :]
    din = jnp.where(diff >= 0, jnp.exp(log_g[:, None, None] * jnp.maximum(diff, 0.0)), 0.0)
    dq = jnp.exp(log_g[:, None] * (idx + 1.0))[..., None]
    dk = jnp.exp(log_g[:, None] * (c - 1.0 - idx))[..., None]
    return jnp.cos(ang), jnp.sin(ang), din, dq, dk


def ret_layer(x2, s, gain, w_in, q_gain, k_gain, gn_w, gn_b, w_out, *, tm):
    t, d = x2.shape
    c = RET_CHUNK
    dqk, dv = RET_QK_DIM, RET_V_DIM
    heads = w_out.shape[0] // dv
    tm = min(tm, s)
    tps = s // tm
    cos, sin, din, dq, dk = _ret_tables(s, heads)
    v_off = 2 * heads * dqk // dv
    g_off = v_off + heads
    return pl.pallas_call(
        functools.partial(_ret_layer_kernel, tiles_per_seq=tps),
        grid=(t // tm, heads),
        in_specs=[
            pl.BlockSpec((tm, d), lambda i, h: (i, 0)),
            pl.BlockSpec((1, d), lambda i, h: (0, 0)),
            pl.BlockSpec((d, dqk), lambda i, h: (0, h)),
            pl.BlockSpec((d, dqk), lambda i, h: (0, heads + h)),
            pl.BlockSpec((d, dv), lambda i, h: (0, v_off + h)),
            pl.BlockSpec((d, dv), lambda i, h: (0, g_off + h)),
            pl.BlockSpec((1, dqk), lambda i, h: (0, 0)),
            pl.BlockSpec((1, dqk), lambda i, h: (0, 0)),
            pl.BlockSpec((tm, dqk // 2), lambda i, h: (i % tps, 0)),
            pl.BlockSpec((tm, dqk // 2), lambda i, h: (i % tps, 0)),
            pl.BlockSpec((1, c, c), lambda i, h: (h, 0, 0)),
            pl.BlockSpec((1, c, 1), lambda i, h: (h, 0, 0)),
            pl.BlockSpec((1, c, 1), lambda i, h: (h, 0, 0)),
            pl.BlockSpec((1, dv), lambda i, h: (0, h)),
            pl.BlockSpec((1, dv), lambda i, h: (0, h)),
            pl.BlockSpec((dv, d), lambda i, h: (h, 0)),
        ],
        out_specs=pl.BlockSpec((tm, d), lambda i, h: (i, 0)),
        out_shape=jax.ShapeDtypeStruct((t, d), F32),
        scratch_shapes=[pltpu.VMEM((tm, d), BF16), pltpu.VMEM((heads, dqk, dv), F32)],
        compiler_params=_params("arbitrary", "arbitrary"),
        name="ret_layer",
    )(x2, gain.reshape(1, d), w_in, w_in, w_in, w_in, q_gain.reshape(1, dqk), k_gain.reshape(1, dqk),
      cos, sin, din, dq, dk, gn_w.reshape(1, heads * dv), gn_b.reshape(1, heads * dv), w_out)


def _sb_core_kernel(q_ref, k_ref, v_ref, g_ref, qg_ref, kg_ref, u_ref, y_ref,
                    qs_ref, ks_ref, acc_ref, run_ref, *, tb, hp):
    d = SB_HEAD_DIM
    s = q_ref.shape[1]
    nt = s // tb
    row = lax.broadcasted_iota(jnp.int32, (tb, tb), 0)
    col = lax.broadcasted_iota(jnp.int32, (tb, tb), 1)
    causal = col < row

    def rms(t, gain):
        return t * lax.rsqrt(jnp.mean(t * t, axis=-1, keepdims=True) + EPS) * gain

    def scores(qi, hh, j0):
        kj = ks_ref[hh, pl.ds(j0, tb), :]
        return lax.dot_general(qi, kj, (((1,), (1,)), ((), ())), preferred_element_type=F32)

    def softplus(z):
        return jnp.maximum(z, 0.0) + jnp.log(1.0 + jnp.exp(-jnp.abs(z)))

    def suffix(sp):
        hi = sp.astype(BF16)
        lo = (sp - hi.astype(F32)).astype(BF16)
        return jnp.dot(jnp.concatenate([hi, lo], axis=1), u_ref[...], preferred_element_type=F32)

    def values(hh, j0):
        return v_ref[0, pl.ds(j0, tb), hh * d:(hh + 1) * d]

    def diag_tile(qi, hh, i0):
        z = scores(qi, hh, i0)
        sp = jnp.where(causal, softplus(z), 0.0)
        suf = suffix(sp)
        w = jnp.where(causal, jnp.exp(z - sp - suf), 0.0)
        acc = jnp.dot(w.astype(BF16), values(hh, i0), preferred_element_type=F32)
        return acc, suf[:, 0:1] + sp[:, 0:1]

    def off_tile(qi, hh, j0, acc, run):
        z = scores(qi, hh, j0)
        sp = softplus(z)
        suf = suffix(sp)
        w = jnp.exp(z - sp - (suf + run))
        acc = acc + jnp.dot(w.astype(BF16), values(hh, j0), preferred_element_type=F32)
        return acc, run + suf[:, 0:1] + sp[:, 0:1]

    def finish(hh, i0, acc):
        g = g_ref[0, pl.ds(i0, tb), hh * d:(hh + 1) * d].astype(F32)
        y_ref[0, pl.ds(i0, tb), hh * d:(hh + 1) * d] = (_silu(g) * acc).astype(y_ref.dtype)

    for hh in range(hp):
        cs = slice(hh * d, (hh + 1) * d)
        qs_ref[hh] = (rms(q_ref[0, :, cs].astype(F32), qg_ref[...]) * (d ** -0.5)).astype(BF16)
        ks_ref[hh] = rms(k_ref[0, :, cs].astype(F32), kg_ref[...]).astype(BF16)

    for hh in range(hp):
        acc, _ = diag_tile(qs_ref[hh, 0:tb, :], hh, 0)
        finish(hh, 0, acc)

    def qtile(i, carry):
        i0 = pl.multiple_of(i * tb, tb)
        rmin = []
        for hh in range(hp):
            qi = qs_ref[hh, pl.ds(i0, tb), :]
            acc, run = diag_tile(qi, hh, i0)
            acc, run = off_tile(qi, hh, pl.multiple_of(i0 - tb, tb), acc, run)
            acc_ref[hh] = acc
            run_ref[hh] = run
            rmin.append(jnp.min(run))

        for hh in range(hp):
            def more(c):
                j, r = c
                return jnp.logical_and(j >= 0, r <= SB_UNDERFLOW)

            def ktile(c, hh=hh):
                j, _ = c
                qi = qs_ref[hh, pl.ds(i0, tb), :]
                a, r = off_tile(qi, hh, pl.multiple_of(j * tb, tb), acc_ref[hh], run_ref[hh])
                acc_ref[hh] = a
                run_ref[hh] = r
                return j - 1, jnp.min(r)

            lax.while_loop(more, ktile, (i - 2, rmin[hh]))

        for hh in range(hp):
            finish(hh, i0, acc_ref[hh])
        return carry

    lax.fori_loop(1, nt, qtile, 0)


def sb_core(p3, q_gain, k_gain, *, tb, hp):
    bsz, s, n4 = p3.shape
    d = SB_HEAD_DIM
    width = n4 // 4
    heads = width // d
    hp = min(hp, heads)
    tb = min(tb, s)
    ng = heads // hp
    u = (jnp.arange(tb)[:, None] > jnp.arange(tb)[None, :]).astype(BF16)
    u = jnp.concatenate([u, u], axis=0)

    def stream(g):
        return pl.BlockSpec((1, s, hp * d), lambda b, h: (b, 0, g * ng + h))

    return pl.pallas_call(
        functools.partial(_sb_core_kernel, tb=tb, hp=hp),
        grid=(bsz, ng),
        in_specs=[stream(0), stream(1), stream(2), stream(3),
                  pl.BlockSpec((1, d), lambda b, h: (0, 0)),
                  pl.BlockSpec((1, d), lambda b, h: (0, 0)),
                  pl.BlockSpec((2 * tb, tb), lambda b, h: (0, 0))],
        out_specs=pl.BlockSpec((1, s, hp * d), lambda b, h: (b, 0, h)),
        out_shape=jax.ShapeDtypeStruct((bsz, s, width), BF16),
        scratch_shapes=[pltpu.VMEM((hp, s, d), BF16), pltpu.VMEM((hp, s, d), BF16),
                        pltpu.VMEM((hp, tb, d), F32), pltpu.VMEM((hp, tb, 1), F32)],
        compiler_params=_params("parallel", "parallel"),
        name="sb_core",
    )(p3, p3, p3, p3, q_gain.reshape(1, d), k_gain.reshape(1, d), u)


def kernel(x, conv_norm, conv_w_in, conv_w, conv_b, conv_w_out, ret_norm, ret_w_in, ret_q_gain, ret_k_gain,
           ret_gn_w, ret_gn_b, ret_w_out, sb_norm, sb_w_in, sb_q_gain, sb_k_gain, sb_w_out):
    bsz, s, d = x.shape
    depth = conv_norm.shape[0] + ret_norm.shape[0] + sb_norm.shape[0]
    x2 = x.reshape(bsz * s, d)
    for i in range(depth):
        kind, j = i % N_MIXERS, i // N_MIXERS
        if kind == 0:
            x2 = conv_layer(x2, s, conv_norm[j], conv_w_in[j].astype(BF16), conv_w[j], conv_b[j],
                            conv_w_out[j].astype(BF16), tm=512, ec=512)
            continue
        elif kind == 1:
            x2 = ret_layer(x2, s, ret_norm[j], ret_w_in[j].astype(BF16), ret_q_gain[j], ret_k_gain[j],
                           ret_gn_w[j], ret_gn_b[j], ret_w_out[j].astype(BF16), tm=512)
            continue
        else:
            p = norm_proj(x2, sb_norm[j], sb_w_in[j].astype(BF16), tm=1024, tn=1024)
            y = sb_core(p.reshape(bsz, s, -1), sb_q_gain[j], sb_k_gain[j], tb=256, hp=2)
            w_out = sb_w_out[j]
        x2 = out_proj(y.reshape(bsz * s, -1), w_out.astype(BF16), x2, tm=1024, tn=1024)
    return x2.reshape(bsz, s, d)
```

```python
import functools

import jax
import jax.numpy as jnp
from jax import lax
from jax.experimental import pallas as pl
from jax.experimental.pallas import tpu as pltpu

EPS = 1e-6
GN_EPS = 1e-5
N_MIXERS = 3
CONV_K = 3
RET_QK_DIM = 256
RET_V_DIM = 512
RET_CHUNK = 128
ROPE_BASE = 10000.0
SB_HEAD_DIM = 128
SB_UNDERFLOW = 104.0

F32 = jnp.float32
BF16 = jnp.bfloat16

VMEM_LIMIT_BYTES = 56 * 1024 * 1024


def _params(*sem):
    return pltpu.CompilerParams(dimension_semantics=sem, vmem_limit_bytes=VMEM_LIMIT_BYTES)


def _silu(g):
    return g * jax.nn.sigmoid(g)


def _rmsnorm_rows(x_ref, g_ref, h_ref, rows, copy_ref=None):
    def body(r, c):
        sl = pl.ds(pl.multiple_of(r * rows, rows), rows)
        x = x_ref[sl, :]
        ms = jnp.mean(x * x, axis=-1, keepdims=True)
        h_ref[sl, :] = (x * lax.rsqrt(ms + EPS) * g_ref[...]).astype(h_ref.dtype)
        if copy_ref is not None:
            copy_ref[sl, :] = x
        return c

    lax.fori_loop(0, x_ref.shape[0] // rows, body, 0)


def _conv_layer_kernel(x_ref, gain_ref, wb_ref, wc_ref, wu_ref, wg_ref, cw_ref, cb_ref, wo_ref, o_ref,
                       h_ref, carry_ref, *, tiles_per_seq):
    i, j = pl.program_id(0), pl.program_id(1)

    @pl.when(j == 0)
    def _():
        _rmsnorm_rows(x_ref, gain_ref, h_ref, 128, copy_ref=o_ref)

    h = h_ref[...]
    v = (jnp.dot(h, wc_ref[...], preferred_element_type=F32)
         * jnp.dot(h, wu_ref[...], preferred_element_type=F32))
    tm = v.shape[0]
    prev = jnp.where(i % tiles_per_seq == 0, 0.0, carry_ref[j])
    carry_ref[j] = v[tm - 8:, :]
    p1, p2 = prev[7:8, :], prev[6:7, :]
    row = lax.broadcasted_iota(jnp.int32, v.shape, 0)
    v1 = jnp.where(row >= 1, pltpu.roll(v, 1, 0), p1)
    v2 = jnp.where(row >= 2, pltpu.roll(v, 2, 0), jnp.where(row == 1, p1, p2))
    conv = cb_ref[...] + v2 * cw_ref[0:1, :]
    conv = conv + v1 * cw_ref[1:2, :]
    conv = conv + v * cw_ref[2:3, :]
    y = jnp.dot(h, wb_ref[...], preferred_element_type=F32) * conv
    y = y * _silu(jnp.dot(h, wg_ref[...], preferred_element_type=F32))
    o_ref[...] += jnp.dot(y.astype(BF16), wo_ref[...], preferred_element_type=F32)


def conv_layer(x2, s, gain, w_in, conv_w, conv_b, w_out, *, tm, ec):
    t, d = x2.shape
    e = w_out.shape[0]
    tm, ec = min(tm, s), min(ec, e)
    nj = e // ec

    def stream(g):
        return pl.BlockSpec((d, ec), lambda i, j: (0, g * nj + j))

    return pl.pallas_call(
        functools.partial(_conv_layer_kernel, tiles_per_seq=s // tm),
        grid=(t // tm, nj),
        in_specs=[pl.BlockSpec((tm, d), lambda i, j: (i, 0)),
                  pl.BlockSpec((1, d), lambda i, j: (0, 0)),
                  stream(0), stream(1), stream(2), stream(3),
                  pl.BlockSpec((CONV_K, ec), lambda i, j: (0, j)),
                  pl.BlockSpec((1, ec), lambda i, j: (0, j)),
                  pl.BlockSpec((ec, d), lambda i, j: (j, 0))],
        out_specs=pl.BlockSpec((tm, d), lambda i, j: (i, 0)),
        out_shape=jax.ShapeDtypeStruct((t, d), F32),
        scratch_shapes=[pltpu.VMEM((tm, d), BF16), pltpu.VMEM((nj, 8, ec), F32)],
        compiler_params=_params("arbitrary", "arbitrary"),
        name="conv_layer",
    )(x2, gain.reshape(1, d), w_in, w_in, w_in, w_in, conv_w, conv_b.reshape(1, e), w_out)


def _ret_layer_kernel(x_ref, gain_ref, wq_ref, wk_ref, wv_ref, wg_ref, qg_ref, kg_ref, cos_ref, sin_ref,
                      din_ref, dq_ref, dk_ref, gw_ref, gb_ref, wo_ref, o_ref, h_ref, state_ref, *, tiles_per_seq):
    i, hd = pl.program_id(0), pl.program_id(1)
    c = RET_CHUNK
    half = RET_QK_DIM // 2

    @pl.when(hd == 0)
    def _():
        _rmsnorm_rows(x_ref, gain_ref, h_ref, 128, copy_ref=o_ref)

    @pl.when(i % tiles_per_seq == 0)
    def _():
        state_ref[hd] = jnp.zeros(state_ref.shape[1:], F32)

    h = h_ref[...]
    cos, sin = cos_ref[...], sin_ref[...]

    def norm_rot(w_ref, gain):
        t = jnp.dot(h, w_ref[...], preferred_element_type=F32)
        t = t * lax.rsqrt(jnp.mean(t * t, axis=-1, keepdims=True) + EPS) * gain
        t1, t2 = t[:, :half], t[:, half:]
        return jnp.concatenate([t1 * cos - t2 * sin, t2 * cos + t1 * sin], axis=-1)

    q = norm_rot(wq_ref, qg_ref[...]).astype(BF16)
    k = norm_rot(wk_ref, kg_ref[...]) * (RET_QK_DIM ** -0.5)
    v = jnp.dot(h, wv_ref[...], preferred_element_type=F32).astype(BF16)
    g = jnp.dot(h, wg_ref[...], preferred_element_type=F32)

    din = din_ref[0]
    dq = dq_ref[0]
    dk = dk_ref[0]
    dchunk = dq[c - 1:c, :]

    ys = []
    for ci in range(q.shape[0] // c):
        sl = slice(ci * c, (ci + 1) * c)
        qi, ki, vi = q[sl], k[sl], v[sl]
        st = state_ref[hd]
        inner = lax.dot_general(qi, ki.astype(BF16), (((1,), (1,)), ((), ())),
                                preferred_element_type=F32) * din
        o = jnp.dot(inner.astype(BF16), vi, preferred_element_type=F32)
        o = o + jnp.dot(qi, st.astype(BF16), preferred_element_type=F32) * dq
        kd = (ki * dk).astype(BF16)
        state_ref[hd] = st * dchunk + lax.dot_general(
            kd, vi, (((0,), (0,)), ((), ())), preferred_element_type=F32)
        mu = jnp.mean(o, axis=-1, keepdims=True)
        var = jnp.mean(jnp.square(o - mu), axis=-1, keepdims=True)
        on = (o - mu) * lax.rsqrt(var + GN_EPS)
        on = on * gw_ref[...] + gb_ref[...]
        ys.append((_silu(g[sl]) * on).astype(BF16))
    y = jnp.concatenate(ys, axis=0)
    o_ref[...] += jnp.dot(y, wo_ref[...], preferred_element_type=F32)


def _ret_tables(s, heads):
    c = RET_CHUNK
    half = RET_QK_DIM // 2
    inv_freq = ROPE_BASE ** (-jnp.arange(half, dtype=F32) / half)
    ang = jnp.arange(s).astype(F32)[:, None] * inv_freq[None, :]
    log_g = jnp.log(1.0 - jnp.exp2(-5.0 - jnp.arange(heads, dtype=F32)))
    idx = jnp.arange(c, dtype=F32)
    diff = idx[:, None] - idx[None, :]
    din = jnp.where(diff >= 0, jnp.exp(log_g[:, None, None] * jnp.maximum(diff, 0.0)), 0.0)
    dq = jnp.exp(log_g[:, None] * (idx + 1.0))[..., None]
    dk = jnp.exp(log_g[:, None] * (c - 1.0 - idx))[..., None]
    return jnp.cos(ang), jnp.sin(ang), din, dq, dk


def ret_layer(x2, s, gain, w_in, q_gain, k_gain, gn_w, gn_b, w_out, *, tm):
    t, d = x2.shape
    c = RET_CHUNK
    dqk, dv = RET_QK_DIM, RET_V_DIM
    heads = w_out.shape[0] // dv
    tm = min(tm, s)
    tps = s // tm
    cos, sin, din, dq, dk = _ret_tables(s, heads)
    v_off = 2 * heads * dqk // dv
    g_off = v_off + heads
    return pl.pallas_call(
        functools.partial(_ret_layer_kernel, tiles_per_seq=tps),
        grid=(t // tm, heads),
        in_specs=[
            pl.BlockSpec((tm, d), lambda i, h: (i, 0)),
            pl.BlockSpec((1, d), lambda i, h: (0, 0)),
            pl.BlockSpec((d, dqk), lambda i, h: (0, h)),
            pl.BlockSpec((d, dqk), lambda i, h: (0, heads + h)),
            pl.BlockSpec((d, dv), lambda i, h: (0, v_off + h)),
            pl.BlockSpec((d, dv), lambda i, h: (0, g_off + h)),
            pl.BlockSpec((1, dqk), lambda i, h: (0, 0)),
            pl.BlockSpec((1, dqk), lambda i, h: (0, 0)),
            pl.BlockSpec((tm, dqk // 2), lambda i, h: (i % tps, 0)),
            pl.BlockSpec((tm, dqk // 2), lambda i, h: (i % tps, 0)),
            pl.BlockSpec((1, c, c), lambda i, h: (h, 0, 0)),
            pl.BlockSpec((1, c, 1), lambda i, h: (h, 0, 0)),
            pl.BlockSpec((1, c, 1), lambda i, h: (h, 0, 0)),
            pl.BlockSpec((1, dv), lambda i, h: (0, h)),
            pl.BlockSpec((1, dv), lambda i, h: (0, h)),
            pl.BlockSpec((dv, d), lambda i, h: (h, 0)),
        ],
        out_specs=pl.BlockSpec((tm, d), lambda i, h: (i, 0)),
        out_shape=jax.ShapeDtypeStruct((t, d), F32),
        scratch_shapes=[pltpu.VMEM((tm, d), BF16), pltpu.VMEM((heads, dqk, dv), F32)],
        compiler_params=_params("arbitrary", "arbitrary"),
        name="ret_layer",
    )(x2, gain.reshape(1, d), w_in, w_in, w_in, w_in, q_gain.reshape(1, dqk), k_gain.reshape(1, dqk),
      cos, sin, din, dq, dk, gn_w.reshape(1, heads * dv), gn_b.reshape(1, heads * dv), w_out)


def _sb_layer_kernel(x_ref, gain_ref, wq_ref, wk_ref, wv_ref, wg_ref, qg_ref, kg_ref, u_ref, wo_ref, o_ref,
                     h_ref, kc0_ref, kc1_ref, vc0_ref, vc1_ref, qs0_ref, qs1_ref, gs0_ref, gs1_ref,
                     acc_ref, run_ref, *, tb, hp, n_pairs, tiles_per_seq):
    i, p = pl.program_id(0), pl.program_id(1)
    d = SB_HEAD_DIM
    tm = x_ref.shape[0]
    nq = tm // tb
    row0 = pl.multiple_of((i % tiles_per_seq) * tm, tm)
    kc, vc, qs, gs = (kc0_ref, kc1_ref), (vc0_ref, vc1_ref), (qs0_ref, qs1_ref), (gs0_ref, gs1_ref)
    row = lax.broadcasted_iota(jnp.int32, (tb, tb), 0)
    col = lax.broadcasted_iota(jnp.int32, (tb, tb), 1)
    causal = col < row

    def rms(t, gain):
        return t * lax.rsqrt(jnp.mean(t * t, axis=-1, keepdims=True) + EPS) * gain

    def project_qk(par, slot):
        h = h_ref[...]
        q = jnp.dot(h, wq_ref[...], preferred_element_type=F32)
        k = jnp.dot(h, wk_ref[...], preferred_element_type=F32)
        for hh in range(hp):
            cs = slice(hh * d, (hh + 1) * d)
            qs[par][hh] = (rms(q[:, cs], qg_ref[...]) * (d ** -0.5)).astype(BF16)
            kc[par][slot * hp + hh, pl.ds(row0, tm), :] = rms(k[:, cs], kg_ref[...]).astype(BF16)

    def project_vg(par, slot):
        h = h_ref[...]
        v = jnp.dot(h, wv_ref[...], preferred_element_type=F32)
        gs[par][...] = jnp.dot(h, wg_ref[...], preferred_element_type=F32)
        for hh in range(hp):
            vc[par][slot * hp + hh, pl.ds(row0, tm), :] = v[:, hh * d:(hh + 1) * d].astype(BF16)

    def softplus(z):
        return jnp.maximum(z, 0.0) + jnp.log(1.0 + jnp.exp(-jnp.abs(z)))

    def suffix(sp):
        hi = sp.astype(BF16)
        lo = (sp - hi.astype(F32)).astype(BF16)
        return jnp.dot(jnp.concatenate([hi, lo], axis=1), u_ref[...], preferred_element_type=F32)

    def scores(qi, par, head, j0):
        kj = kc[par][head, pl.ds(j0, tb), :]
        return lax.dot_general(qi, kj, (((1,), (1,)), ((), ())), preferred_element_type=F32)

    def off_tile(qi, par, head, j0, acc, run):
        z = scores(qi, par, head, j0)
        sp = softplus(z)
        suf = suffix(sp)
        w = jnp.exp(z - sp - (suf + run))
        acc = acc + jnp.dot(w.astype(BF16), vc[par][head, pl.ds(j0, tb), :], preferred_element_type=F32)
        return acc, run + suf[:, 0:1] + sp[:, 0:1]

    def attend_first(par, slot, between):
        chains = [(qt, hh) for qt in range(nq) for hh in range(hp)]
        r0 = [pl.multiple_of(row0 + qt * tb, tb) for qt in range(nq)]
        j0 = [pl.multiple_of(jnp.maximum(r - tb, 0), tb) for r in r0]
        zd, zo, spd, spo, sufd, sufo = {}, {}, {}, {}, {}, {}
        for c in chains:
            qt, hh = c
            qi = qs[par][hh, qt * tb:(qt + 1) * tb, :]
            zd[c] = scores(qi, par, slot * hp + hh, r0[qt])
            zo[c] = scores(qi, par, slot * hp + hh, j0[qt])
        between[0]()
        for c in chains:
            spd[c] = jnp.where(causal, softplus(zd[c]), 0.0)
            spo[c] = softplus(zo[c])
            sufd[c] = suffix(spd[c])
            sufo[c] = suffix(spo[c])
        between[1]()
        rmin = []
        for c in chains:
            qt, hh = c
            head = slot * hp + hh
            run = sufd[c][:, 0:1] + spd[c][:, 0:1]
            wd = jnp.where(causal, jnp.exp(zd[c] - spd[c] - sufd[c]), 0.0)
            acc = jnp.dot(wd.astype(BF16), vc[par][head, pl.ds(r0[qt], tb), :], preferred_element_type=F32)
            wo = jnp.exp(zo[c] - spo[c] - (sufo[c] + run))
            acc2 = acc + jnp.dot(wo.astype(BF16), vc[par][head, pl.ds(j0[qt], tb), :],
                                 preferred_element_type=F32)
            run2 = run + sufo[c][:, 0:1] + spo[c][:, 0:1]
            if qt == 0:
                acc2 = jnp.where(r0[qt] > 0, acc2, acc)
                run2 = jnp.where(r0[qt] > 0, run2, run)
            acc_ref[qt, hh] = acc2
            run_ref[qt, hh] = run2
            rmin.append(jnp.min(run2))
        return rmin

    def attend_rest(par, slot, rmin):
        for qt in range(nq):
            r0 = pl.multiple_of(row0 + qt * tb, tb)
            for hh in range(hp):
                def more(c):
                    j, r = c
                    return jnp.logical_and(j >= 0, r <= SB_UNDERFLOW)

                def ktile(c, qt=qt, hh=hh):
                    j, _ = c
                    qi = qs[par][hh, qt * tb:(qt + 1) * tb, :]
                    a, r = off_tile(qi, par, slot * hp + hh, pl.multiple_of(j * tb, tb),
                                    acc_ref[qt, hh], run_ref[qt, hh])
                    acc_ref[qt, hh] = a
                    run_ref[qt, hh] = r
                    return j - 1, jnp.min(r)

                lax.while_loop(more, ktile, (r0 // tb - 2, rmin[qt * hp + hh]))

    def finish(par):
        g = gs[par][...]
        y = jnp.concatenate(
            [jnp.concatenate([acc_ref[qt, hh] for hh in range(hp)], axis=1) for qt in range(nq)], axis=0)
        y = (_silu(g) * y).astype(BF16)
        o_ref[...] += jnp.dot(y, wo_ref[...], preferred_element_type=F32)

    def step(attend_par, project_par):
        if project_par is None:
            between = (lambda: None, lambda: None)
        else:
            between = (functools.partial(project_qk, project_par, p // 2),
                       functools.partial(project_vg, project_par, p // 2))
        if attend_par is None:
            between[0]()
            between[1]()
        else:
            a_slot = (p - 1) // 2
            rmin = attend_first(attend_par, a_slot, between)
            attend_rest(attend_par, a_slot, rmin)
            finish(attend_par)

    last_par = (n_pairs - 1) % 2

    @pl.when(p == 0)
    def _():
        _rmsnorm_rows(x_ref, gain_ref, h_ref, 128, copy_ref=o_ref)
        step(None, 0)

    @pl.when(jnp.logical_and(jnp.logical_and(p > 0, p < n_pairs), p % 2 == 1))
    def _():
        step(0, 1)

    @pl.when(jnp.logical_and(jnp.logical_and(p > 0, p < n_pairs), p % 2 == 0))
    def _():
        step(1, 0)

    @pl.when(p == n_pairs)
    def _():
        step(last_par, None)


def sb_layer(x2, s, gain, w_in, q_gain, k_gain, w_out, *, tm, tb, hp):
    t, dm = x2.shape
    d = SB_HEAD_DIM
    width = w_out.shape[0]
    heads = width // d
    tm, tb = min(tm, s), min(tb, s)
    n_pairs = heads // hp
    half = (n_pairs + 1) // 2
    u = (jnp.arange(tb)[:, None] > jnp.arange(tb)[None, :]).astype(BF16)
    u = jnp.concatenate([u, u], axis=0)

    def stream(g):
        return pl.BlockSpec((dm, hp * d), lambda i, p: (0, g * n_pairs + jnp.minimum(p, n_pairs - 1)))

    cache = pltpu.VMEM((half * hp, s, d), BF16)
    return pl.pallas_call(
        functools.partial(_sb_layer_kernel, tb=tb, hp=hp, n_pairs=n_pairs, tiles_per_seq=s // tm),
        grid=(t // tm, n_pairs + 1),
        in_specs=[pl.BlockSpec((tm, dm), lambda i, p: (i, 0)),
                  pl.BlockSpec((1, dm), lambda i, p: (0, 0)),
                  stream(0), stream(1), stream(2), stream(3),
                  pl.BlockSpec((1, d), lambda i, p: (0, 0)),
                  pl.BlockSpec((1, d), lambda i, p: (0, 0)),
                  pl.BlockSpec((2 * tb, tb), lambda i, p: (0, 0)),
                  pl.BlockSpec((hp * d, dm), lambda i, p: (jnp.maximum(p - 1, 0), 0))],
        out_specs=pl.BlockSpec((tm, dm), lambda i, p: (i, 0)),
        out_shape=jax.ShapeDtypeStruct((t, dm), F32),
        scratch_shapes=[pltpu.VMEM((tm, dm), BF16), cache, cache, cache, cache,
                        pltpu.VMEM((hp, tm, d), BF16), pltpu.VMEM((hp, tm, d), BF16),
                        pltpu.VMEM((tm, hp * d), F32), pltpu.VMEM((tm, hp * d), F32),
                        pltpu.VMEM((tm // tb, hp, tb, d), F32), pltpu.VMEM((tm // tb, hp, tb, 1), F32)],
        compiler_params=_params("arbitrary", "arbitrary"),
        name="sb_layer",
    )(x2, gain.reshape(1, dm), w_in, w_in, w_in, w_in, q_gain.reshape(1, d), k_gain.reshape(1, d), u, w_out)


def kernel(x, conv_norm, conv_w_in, conv_w, conv_b, conv_w_out, ret_norm, ret_w_in, ret_q_gain, ret_k_gain,
           ret_gn_w, ret_gn_b, ret_w_out, sb_norm, sb_w_in, sb_q_gain, sb_k_gain, sb_w_out):
    bsz, s, d = x.shape
    depth = conv_norm.shape[0] + ret_norm.shape[0] + sb_norm.shape[0]
    x2 = x.reshape(bsz * s, d)
    for i in range(depth):
        kind, j = i % N_MIXERS, i // N_MIXERS
        if kind == 0:
            x2 = conv_layer(x2, s, conv_norm[j], conv_w_in[j].astype(BF16), conv_w[j], conv_b[j],
                            conv_w_out[j].astype(BF16), tm=512, ec=512)
        elif kind == 1:
            x2 = ret_layer(x2, s, ret_norm[j], ret_w_in[j].astype(BF16), ret_q_gain[j], ret_k_gain[j],
                           ret_gn_w[j], ret_gn_b[j], ret_w_out[j].astype(BF16), tm=512)
        else:
            x2 = sb_layer(x2, s, sb_norm[j], sb_w_in[j].astype(BF16), sb_q_gain[j], sb_k_gain[j],
                          sb_w_out[j].astype(BF16), tm=512, tb=256, hp=2)
    return x2.reshape(bsz, s, d)
```

```python
import functools

import jax
import jax.numpy as jnp
from jax import lax
from jax.experimental import pallas as pl
from jax.experimental.pallas import tpu as pltpu

EPS = 1e-6
GN_EPS = 1e-5
N_MIXERS = 3
CONV_K = 3
RET_QK_DIM = 256
RET_V_DIM = 512
RET_CHUNK = 128
ROPE_BASE = 10000.0
SB_HEAD_DIM = 128
SB_UNDERFLOW = 104.0

F32 = jnp.float32
BF16 = jnp.bfloat16

VMEM_LIMIT_BYTES = 56 * 1024 * 1024


def _params(*sem):
    return pltpu.CompilerParams(dimension_semantics=sem, vmem_limit_bytes=VMEM_LIMIT_BYTES)


def _silu(g):
    return g * jax.nn.sigmoid(g)


def _group_columns(w, stream_widths, groups):
    d = w.shape[0]
    parts, off = [], 0
    for sw in stream_widths:
        parts.append(w[:, off:off + sw].reshape(d, groups, sw // groups))
        off += sw
    return jnp.concatenate(parts, axis=2).transpose(1, 0, 2).astype(BF16)


def _rmsnorm_rows(x_ref, g_ref, h_ref, rows, copy_ref=None):
    def body(r, c):
        sl = pl.ds(pl.multiple_of(r * rows, rows), rows)
        x = x_ref[sl, :]
        ms = jnp.mean(x * x, axis=-1, keepdims=True)
        h_ref[sl, :] = (x * lax.rsqrt(ms + EPS) * g_ref[...]).astype(h_ref.dtype)
        if copy_ref is not None:
            copy_ref[sl, :] = x
        return c

    lax.fori_loop(0, x_ref.shape[0] // rows, body, 0)


def _conv_layer_kernel(x_ref, gain_ref, w_ref, cw_ref, cb_ref, wo_ref, o_ref, h_ref, carry_ref, *, tiles_per_seq):
    i, j = pl.program_id(0), pl.program_id(1)

    @pl.when(j == 0)
    def _():
        _rmsnorm_rows(x_ref, gain_ref, h_ref, 128, copy_ref=o_ref)

    h = h_ref[...]
    ec = cw_ref.shape[1]

    def proj(g):
        return jnp.dot(h, w_ref[0, :, g * ec:(g + 1) * ec], preferred_element_type=F32)

    v = proj(1) * proj(2)
    tm = v.shape[0]
    prev = jnp.where(i % tiles_per_seq == 0, 0.0, carry_ref[j])
    carry_ref[j] = v[tm - 8:, :]
    p1, p2 = prev[7:8, :], prev[6:7, :]
    row = lax.broadcasted_iota(jnp.int32, v.shape, 0)
    v1 = jnp.where(row >= 1, pltpu.roll(v, 1, 0), p1)
    v2 = jnp.where(row >= 2, pltpu.roll(v, 2, 0), jnp.where(row == 1, p1, p2))
    conv = cb_ref[...] + v2 * cw_ref[0:1, :]
    conv = conv + v1 * cw_ref[1:2, :]
    conv = conv + v * cw_ref[2:3, :]
    y = proj(0) * conv
    y = y * _silu(proj(3))
    o_ref[...] += jnp.dot(y.astype(BF16), wo_ref[...], preferred_element_type=F32)


def conv_layer(x2, s, gain, w_in, conv_w, conv_b, w_out, *, tm, ec):
    t, d = x2.shape
    e = w_out.shape[0]
    tm, ec = min(tm, s), min(ec, e)
    nj = e // ec
    return pl.pallas_call(
        functools.partial(_conv_layer_kernel, tiles_per_seq=s // tm),
        grid=(t // tm, nj),
        in_specs=[pl.BlockSpec((tm, d), lambda i, j: (i, 0)),
                  pl.BlockSpec((1, d), lambda i, j: (0, 0)),
                  pl.BlockSpec((1, d, 4 * ec), lambda i, j: (j, 0, 0)),
                  pl.BlockSpec((CONV_K, ec), lambda i, j: (0, j)),
                  pl.BlockSpec((1, ec), lambda i, j: (0, j)),
                  pl.BlockSpec((ec, d), lambda i, j: (j, 0))],
        out_specs=pl.BlockSpec((tm, d), lambda i, j: (i, 0)),
        out_shape=jax.ShapeDtypeStruct((t, d), F32),
        scratch_shapes=[pltpu.VMEM((tm, d), BF16), pltpu.VMEM((nj, 8, ec), F32)],
        compiler_params=_params("arbitrary", "arbitrary"),
        name="conv_layer",
    )(x2, gain.reshape(1, d), _group_columns(w_in, (e,) * 4, nj), conv_w, conv_b.reshape(1, e), w_out.astype(BF16))


def _ret_layer_kernel(x_ref, gain_ref, w_ref, qg_ref, kg_ref, cos_ref, sin_ref,
                      din_ref, dq_ref, dk_ref, gw_ref, gb_ref, wo_ref, o_ref, h_ref, state_ref, *, tiles_per_seq):
    i, hd = pl.program_id(0), pl.program_id(1)
    c = RET_CHUNK
    dqk, dv = RET_QK_DIM, RET_V_DIM
    half = dqk // 2

    @pl.when(hd == 0)
    def _():
        _rmsnorm_rows(x_ref, gain_ref, h_ref, 128, copy_ref=o_ref)

    @pl.when(i % tiles_per_seq == 0)
    def _():
        state_ref[hd] = jnp.zeros(state_ref.shape[1:], F32)

    h = h_ref[...]
    cos, sin = cos_ref[...], sin_ref[...]

    def proj(lo, width):
        return jnp.dot(h, w_ref[0, :, lo:lo + width], preferred_element_type=F32)

    def norm_rot(lo, gain):
        t = proj(lo, dqk)
        t = t * lax.rsqrt(jnp.mean(t * t, axis=-1, keepdims=True) + EPS) * gain
        t1, t2 = t[:, :half], t[:, half:]
        return jnp.concatenate([t1 * cos - t2 * sin, t2 * cos + t1 * sin], axis=-1)

    q = norm_rot(0, qg_ref[...]).astype(BF16)
    k = norm_rot(dqk, kg_ref[...]) * (dqk ** -0.5)
    v = proj(2 * dqk, dv).astype(BF16)
    g = proj(2 * dqk + dv, dv)

    din = din_ref[0]
    dq = dq_ref[0]
    dk = dk_ref[0]
    dchunk = dq[c - 1:c, :]

    ys = []
    for ci in range(q.shape[0] // c):
        sl = slice(ci * c, (ci + 1) * c)
        qi, ki, vi = q[sl], k[sl], v[sl]
        st = state_ref[hd]
        inner = lax.dot_general(qi, ki.astype(BF16), (((1,), (1,)), ((), ())),
                                preferred_element_type=F32) * din
        o = jnp.dot(inner.astype(BF16), vi, preferred_element_type=F32)
        o = o + jnp.dot(qi, st.astype(BF16), preferred_element_type=F32) * dq
        kd = (ki * dk).astype(BF16)
        state_ref[hd] = st * dchunk + lax.dot_general(
            kd, vi, (((0,), (0,)), ((), ())), preferred_element_type=F32)
        mu = jnp.mean(o, axis=-1, keepdims=True)
        var = jnp.mean(jnp.square(o - mu), axis=-1, keepdims=True)
        on = (o - mu) * lax.rsqrt(var + GN_EPS)
        on = on * gw_ref[...] + gb_ref[...]
        ys.append((_silu(g[sl]) * on).astype(BF16))
    y = jnp.concatenate(ys, axis=0)
    o_ref[...] += jnp.dot(y, wo_ref[...], preferred_element_type=F32)


def _ret_tables(s, heads):
    c = RET_CHUNK
    half = RET_QK_DIM // 2
    inv_freq = ROPE_BASE ** (-jnp.arange(half, dtype=F32) / half)
    ang = jnp.arange(s).astype(F32)[:, None] * inv_freq[None, :]
    log_g = jnp.log(1.0 - jnp.exp2(-5.0 - jnp.arange(heads, dtype=F32)))
    idx = jnp.arange(c, dtype=F32)
    diff = idx[:, None] - idx[None, :]
    din = jnp.where(diff >= 0, jnp.exp(log_g[:, None, None] * jnp.maximum(diff, 0.0)), 0.0)
    dq = jnp.exp(log_g[:, None] * (idx + 1.0))[..., None]
    dk = jnp.exp(log_g[:, None] * (c - 1.0 - idx))[..., None]
    return jnp.cos(ang), jnp.sin(ang), din, dq, dk


def ret_layer(x2, s, gain, w_in, q_gain, k_gain, gn_w, gn_b, w_out, *, tm):
    t, d = x2.shape
    c = RET_CHUNK
    dqk, dv = RET_QK_DIM, RET_V_DIM
    heads = w_out.shape[0] // dv
    tm = min(tm, s)
    tps = s // tm
    cos, sin, din, dq, dk = _ret_tables(s, heads)
    wcols = 2 * dqk + 2 * dv
    return pl.pallas_call(
        functools.partial(_ret_layer_kernel, tiles_per_seq=tps),
        grid=(t // tm, heads),
        in_specs=[
            pl.BlockSpec((tm, d), lambda i, h: (i, 0)),
            pl.BlockSpec((1, d), lambda i, h: (0, 0)),
            pl.BlockSpec((1, d, wcols), lambda i, h: (h, 0, 0)),
            pl.BlockSpec((1, dqk), lambda i, h: (0, 0)),
            pl.BlockSpec((1, dqk), lambda i, h: (0, 0)),
            pl.BlockSpec((tm, dqk // 2), lambda i, h: (i % tps, 0)),
            pl.BlockSpec((tm, dqk // 2), lambda i, h: (i % tps, 0)),
            pl.BlockSpec((1, c, c), lambda i, h: (h, 0, 0)),
            pl.BlockSpec((1, c, 1), lambda i, h: (h, 0, 0)),
            pl.BlockSpec((1, c, 1), lambda i, h: (h, 0, 0)),
            pl.BlockSpec((1, dv), lambda i, h: (0, h)),
            pl.BlockSpec((1, dv), lambda i, h: (0, h)),
            pl.BlockSpec((dv, d), lambda i, h: (h, 0)),
        ],
        out_specs=pl.BlockSpec((tm, d), lambda i, h: (i, 0)),
        out_shape=jax.ShapeDtypeStruct((t, d), F32),
        scratch_shapes=[pltpu.VMEM((tm, d), BF16), pltpu.VMEM((heads, dqk, dv), F32)],
        compiler_params=_params("arbitrary", "arbitrary"),
        name="ret_layer",
    )(x2, gain.reshape(1, d), _group_columns(w_in, (heads * dqk, heads * dqk, heads * dv, heads * dv), heads),
      q_gain.reshape(1, dqk), k_gain.reshape(1, dqk), cos, sin, din, dq, dk,
      gn_w.reshape(1, heads * dv), gn_b.reshape(1, heads * dv), w_out.astype(BF16))


def _sb_layer_kernel(x_ref, gain_ref, w_ref, qg_ref, kg_ref, u_ref, wo_ref, o_ref,
                     h_ref, kc0_ref, kc1_ref, vc0_ref, vc1_ref, qs0_ref, qs1_ref, gs0_ref, gs1_ref,
                     acc_ref, run_ref, *, tb, hp, n_pairs, tiles_per_seq):
    i, p = pl.program_id(0), pl.program_id(1)
    d = SB_HEAD_DIM
    tm = x_ref.shape[0]
    nq = tm // tb
    row0 = pl.multiple_of((i % tiles_per_seq) * tm, tm)
    kc, vc, qs, gs = (kc0_ref, kc1_ref), (vc0_ref, vc1_ref), (qs0_ref, qs1_ref), (gs0_ref, gs1_ref)
    row = lax.broadcasted_iota(jnp.int32, (tb, tb), 0)
    col = lax.broadcasted_iota(jnp.int32, (tb, tb), 1)
    causal = col < row

    def rms(t, gain):
        return t * lax.rsqrt(jnp.mean(t * t, axis=-1, keepdims=True) + EPS) * gain

    def proj(g):
        return jnp.dot(h_ref[...], w_ref[0, :, g * hp * d:(g + 1) * hp * d], preferred_element_type=F32)

    def project_qk(par, slot):
        q, k = proj(0), proj(1)
        for hh in range(hp):
            cs = slice(hh * d, (hh + 1) * d)
            qs[par][hh] = (rms(q[:, cs], qg_ref[...]) * (d ** -0.5)).astype(BF16)
            kc[par][slot * hp + hh, pl.ds(row0, tm), :] = rms(k[:, cs], kg_ref[...]).astype(BF16)

    def project_vg(par, slot):
        v = proj(2)
        gs[par][...] = proj(3)
        for hh in range(hp):
            vc[par][slot * hp + hh, pl.ds(row0, tm), :] = v[:, hh * d:(hh + 1) * d].astype(BF16)

    def softplus(z):
        return jnp.maximum(z, 0.0) + jnp.log(1.0 + jnp.exp(-jnp.abs(z)))

    def suffix(sp):
        hi = sp.astype(BF16)
        lo = (sp - hi.astype(F32)).astype(BF16)
        return jnp.dot(jnp.concatenate([hi, lo], axis=1), u_ref[...], preferred_element_type=F32)

    def scores(qi, par, head, j0):
        kj = kc[par][head, pl.ds(j0, tb), :]
        return lax.dot_general(qi, kj, (((1,), (1,)), ((), ())), preferred_element_type=F32)

    def off_tile(qi, par, head, j0, acc, run):
        z = scores(qi, par, head, j0)
        sp = softplus(z)
        suf = suffix(sp)
        w = jnp.exp(z - sp - (suf + run))
        acc = acc + jnp.dot(w.astype(BF16), vc[par][head, pl.ds(j0, tb), :], preferred_element_type=F32)
        return acc, run + suf[:, 0:1] + sp[:, 0:1]

    def attend_first(par, slot, between):
        chains = [(qt, hh) for qt in range(nq) for hh in range(hp)]
        r0 = [pl.multiple_of(row0 + qt * tb, tb) for qt in range(nq)]
        j0 = [pl.multiple_of(jnp.maximum(r - tb, 0), tb) for r in r0]
        zd, zo, spd, spo, sufd, sufo = {}, {}, {}, {}, {}, {}
        for c in chains:
            qt, hh = c
            qi = qs[par][hh, qt * tb:(qt + 1) * tb, :]
            zd[c] = scores(qi, par, slot * hp + hh, r0[qt])
            zo[c] = scores(qi, par, slot * hp + hh, j0[qt])
        between[0]()
        for c in chains:
            spd[c] = jnp.where(causal, softplus(zd[c]), 0.0)
            spo[c] = softplus(zo[c])
            sufd[c] = suffix(spd[c])
            sufo[c] = suffix(spo[c])
        between[1]()
        rmin = []
        for c in chains:
            qt, hh = c
            head = slot * hp + hh
            run = sufd[c][:, 0:1] + spd[c][:, 0:1]
            wd = jnp.where(causal, jnp.exp(zd[c] - spd[c] - sufd[c]), 0.0)
            acc = jnp.dot(wd.astype(BF16), vc[par][head, pl.ds(r0[qt], tb), :], preferred_element_type=F32)
            wo = jnp.exp(zo[c] - spo[c] - (sufo[c] + run))
            acc2 = acc + jnp.dot(wo.astype(BF16), vc[par][head, pl.ds(j0[qt], tb), :],
                                 preferred_element_type=F32)
            run2 = run + sufo[c][:, 0:1] + spo[c][:, 0:1]
            if qt == 0:
                acc2 = jnp.where(r0[qt] > 0, acc2, acc)
                run2 = jnp.where(r0[qt] > 0, run2, run)
            acc_ref[qt, hh] = acc2
            run_ref[qt, hh] = run2
            rmin.append(jnp.min(run2))
        return rmin

    def attend_rest(par, slot, rmin):
        for qt in range(nq):
            r0 = pl.multiple_of(row0 + qt * tb, tb)
            for hh in range(hp):
                def more(c):
                    j, r = c
                    return jnp.logical_and(j >= 0, r <= SB_UNDERFLOW)

                def ktile(c, qt=qt, hh=hh):
                    j, _ = c
                    qi = qs[par][hh, qt * tb:(qt + 1) * tb, :]
                    a, r = off_tile(qi, par, slot * hp + hh, pl.multiple_of(j * tb, tb),
                                    acc_ref[qt, hh], run_ref[qt, hh])
                    acc_ref[qt, hh] = a
                    run_ref[qt, hh] = r
                    return j - 1, jnp.min(r)

                lax.while_loop(more, ktile, (r0 // tb - 2, rmin[qt * hp + hh]))

    def finish(par):
        g = gs[par][...]
        y = jnp.concatenate(
            [jnp.concatenate([acc_ref[qt, hh] for hh in range(hp)], axis=1) for qt in range(nq)], axis=0)
        y = (_silu(g) * y).astype(BF16)
        o_ref[...] += jnp.dot(y, wo_ref[...], preferred_element_type=F32)

    def step(attend_par, project_par):
        if project_par is None:
            between = (lambda: None, lambda: None)
        else:
            between = (functools.partial(project_qk, project_par, p // 2),
                       functools.partial(project_vg, project_par, p // 2))
        if attend_par is None:
            between[0]()
            between[1]()
        else:
            a_slot = (p - 1) // 2
            rmin = attend_first(attend_par, a_slot, between)
            attend_rest(attend_par, a_slot, rmin)
            finish(attend_par)

    last_par = (n_pairs - 1) % 2

    @pl.when(p == 0)
    def _():
        _rmsnorm_rows(x_ref, gain_ref, h_ref, 128, copy_ref=o_ref)
        step(None, 0)

    @pl.when(jnp.logical_and(jnp.logical_and(p > 0, p < n_pairs), p % 2 == 1))
    def _():
        step(0, 1)

    @pl.when(jnp.logical_and(jnp.logical_and(p > 0, p < n_pairs), p % 2 == 0))
    def _():
        step(1, 0)

    @pl.when(p == n_pairs)
    def _():
        step(last_par, None)


def sb_layer(x2, s, gain, w_in, q_gain, k_gain, w_out, *, tm, tb, hp):
    t, dm = x2.shape
    d = SB_HEAD_DIM
    width = w_out.shape[0]
    heads = width // d
    tm, tb = min(tm, s), min(tb, s)
    n_pairs = heads // hp
    half = (n_pairs + 1) // 2
    u = (jnp.arange(tb)[:, None] > jnp.arange(tb)[None, :]).astype(BF16)
    u = jnp.concatenate([u, u], axis=0)

    cache = pltpu.VMEM((half * hp, s, d), BF16)
    return pl.pallas_call(
        functools.partial(_sb_layer_kernel, tb=tb, hp=hp, n_pairs=n_pairs, tiles_per_seq=s // tm),
        grid=(t // tm, n_pairs + 1),
        in_specs=[pl.BlockSpec((tm, dm), lambda i, p: (i, 0)),
                  pl.BlockSpec((1, dm), lambda i, p: (0, 0)),
                  pl.BlockSpec((1, dm, 4 * hp * d), lambda i, p: (jnp.minimum(p, n_pairs - 1), 0, 0)),
                  pl.BlockSpec((1, d), lambda i, p: (0, 0)),
                  pl.BlockSpec((1, d), lambda i, p: (0, 0)),
                  pl.BlockSpec((2 * tb, tb), lambda i, p: (0, 0)),
                  pl.BlockSpec((hp * d, dm), lambda i, p: (jnp.maximum(p - 1, 0), 0))],
        out_specs=pl.BlockSpec((tm, dm), lambda i, p: (i, 0)),
        out_shape=jax.ShapeDtypeStruct((t, dm), F32),
        scratch_shapes=[pltpu.VMEM((tm, dm), BF16), cache, cache, cache, cache,
                        pltpu.VMEM((hp, tm, d), BF16), pltpu.VMEM((hp, tm, d), BF16),
                        pltpu.VMEM((tm, hp * d), F32), pltpu.VMEM((tm, hp * d), F32),
                        pltpu.VMEM((tm // tb, hp, tb, d), F32), pltpu.VMEM((tm // tb, hp, tb, 1), F32)],
        compiler_params=_params("arbitrary", "arbitrary"),
        name="sb_layer",
    )(x2, gain.reshape(1, dm), _group_columns(w_in, (width,) * 4, n_pairs), q_gain.reshape(1, d), k_gain.reshape(1, d),
      u, w_out.astype(BF16))


def kernel(x, conv_norm, conv_w_in, conv_w, conv_b, conv_w_out, ret_norm, ret_w_in, ret_q_gain, ret_k_gain,
           ret_gn_w, ret_gn_b, ret_w_out, sb_norm, sb_w_in, sb_q_gain, sb_k_gain, sb_w_out):
    bsz, s, d = x.shape
    depth = conv_norm.shape[0] + ret_norm.shape[0] + sb_norm.shape[0]
    x2 = x.reshape(bsz * s, d)
    for i in range(depth):
        kind, j = i % N_MIXERS, i // N_MIXERS
        if kind == 0:
            x2 = conv_layer(x2, s, conv_norm[j], conv_w_in[j], conv_w[j], conv_b[j], conv_w_out[j], tm=512, ec=512)
        elif kind == 1:
            x2 = ret_layer(x2, s, ret_norm[j], ret_w_in[j], ret_q_gain[j], ret_k_gain[j],
                           ret_gn_w[j], ret_gn_b[j], ret_w_out[j], tm=512)
        else:
            x2 = sb_layer(x2, s, sb_norm[j], sb_w_in[j], sb_q_gain[j], sb_k_gain[j], sb_w_out[j], tm=512, tb=256, hp=2)
    return x2.reshape(bsz, s, d)
```

```python
import functools

import jax
import jax.numpy as jnp
from jax import lax
from jax.experimental import pallas as pl
from jax.experimental.pallas import tpu as pltpu

EPS = 1e-6
GN_EPS = 1e-5
N_MIXERS = 3
CONV_K = 3
RET_QK_DIM = 256
RET_V_DIM = 512
RET_CHUNK = 128
ROPE_BASE = 10000.0
SB_HEAD_DIM = 128
SB_UNDERFLOW = 104.0

F32 = jnp.float32
BF16 = jnp.bfloat16

VMEM_LIMIT_BYTES = 56 * 1024 * 1024


def _params(*sem):
    return pltpu.CompilerParams(dimension_semantics=sem, vmem_limit_bytes=VMEM_LIMIT_BYTES)


def _silu(g):
    return g * jax.nn.sigmoid(g)


def _rmsnorm_rows(x_ref, g_ref, h_ref, rows, copy_ref=None):
    def body(r, c):
        sl = pl.ds(pl.multiple_of(r * rows, rows), rows)
        x = x_ref[sl, :]
        ms = jnp.mean(x * x, axis=-1, keepdims=True)
        h_ref[sl, :] = (x * lax.rsqrt(ms + EPS) * g_ref[...]).astype(h_ref.dtype)
        if copy_ref is not None:
            copy_ref[sl, :] = x
        return c

    lax.fori_loop(0, x_ref.shape[0] // rows, body, 0)


def _conv_layer_kernel(x_ref, gain_ref, wb_ref, wc_ref, wu_ref, wg_ref, cw_ref, cb_ref, wo_ref, o_ref,
                       h_ref, carry_ref, *, tiles_per_seq):
    i, j = pl.program_id(0), pl.program_id(1)

    @pl.when(j == 0)
    def _():
        _rmsnorm_rows(x_ref, gain_ref, h_ref, 128, copy_ref=o_ref)

    h = h_ref[...]
    v = (jnp.dot(h, wc_ref[...], preferred_element_type=F32)
         * jnp.dot(h, wu_ref[...], preferred_element_type=F32))
    gate = _silu(jnp.dot(h, wg_ref[...], preferred_element_type=F32))
    tm = v.shape[0]
    prev = jnp.where(i % tiles_per_seq == 0, 0.0, carry_ref[j])
    carry_ref[j] = v[tm - 8:, :]
    p1, p2 = prev[7:8, :], prev[6:7, :]
    row = lax.broadcasted_iota(jnp.int32, v.shape, 0)
    v1 = jnp.where(row >= 1, pltpu.roll(v, 1, 0), p1)
    v2 = jnp.where(row >= 2, pltpu.roll(v, 2, 0), jnp.where(row == 1, p1, p2))
    conv = cb_ref[...] + v2 * cw_ref[0:1, :]
    conv = conv + v1 * cw_ref[1:2, :]
    conv = conv + v * cw_ref[2:3, :]
    y = jnp.dot(h, wb_ref[...], preferred_element_type=F32) * conv
    y = y * gate
    o_ref[...] += jnp.dot(y.astype(BF16), wo_ref[...], preferred_element_type=F32)


def conv_layer(x2, s, gain, w_in, conv_w, conv_b, w_out, *, tm, ec):
    t, d = x2.shape
    e = w_out.shape[0]
    tm, ec = min(tm, s), min(ec, e)
    nj = e // ec

    def stream(g):
        return pl.BlockSpec((d, ec), lambda i, j: (0, g * nj + j))

    return pl.pallas_call(
        functools.partial(_conv_layer_kernel, tiles_per_seq=s // tm),
        grid=(t // tm, nj),
        in_specs=[pl.BlockSpec((tm, d), lambda i, j: (i, 0)),
                  pl.BlockSpec((1, d), lambda i, j: (0, 0)),
                  stream(0), stream(1), stream(2), stream(3),
                  pl.BlockSpec((CONV_K, ec), lambda i, j: (0, j)),
                  pl.BlockSpec((1, ec), lambda i, j: (0, j)),
                  pl.BlockSpec((ec, d), lambda i, j: (j, 0))],
        out_specs=pl.BlockSpec((tm, d), lambda i, j: (i, 0)),
        out_shape=jax.ShapeDtypeStruct((t, d), F32),
        scratch_shapes=[pltpu.VMEM((tm, d), BF16), pltpu.VMEM((nj, 8, ec), F32)],
        compiler_params=_params("arbitrary", "arbitrary"),
        name="conv_layer",
    )(x2, gain.reshape(1, d), w_in, w_in, w_in, w_in, conv_w, conv_b.reshape(1, e), w_out)


def _ret_layer_kernel(x_ref, gain_ref, wq_ref, wk_ref, wv_ref, wg_ref, qg_ref, kg_ref, cos_ref, sin_ref,
                      din_ref, dq_ref, dk_ref, gw_ref, gb_ref, wo_ref, o_ref, h_ref, state_ref, *, tiles_per_seq):
    i, hd = pl.program_id(0), pl.program_id(1)
    c = RET_CHUNK
    half = RET_QK_DIM // 2

    @pl.when(hd == 0)
    def _():
        _rmsnorm_rows(x_ref, gain_ref, h_ref, 128, copy_ref=o_ref)

    @pl.when(i % tiles_per_seq == 0)
    def _():
        state_ref[hd] = jnp.zeros(state_ref.shape[1:], F32)

    h = h_ref[...]
    cos, sin = cos_ref[...], sin_ref[...]

    def norm_rot(w_ref, gain):
        t = jnp.dot(h, w_ref[...], preferred_element_type=F32)
        t = t * lax.rsqrt(jnp.mean(t * t, axis=-1, keepdims=True) + EPS) * gain
        t1, t2 = t[:, :half], t[:, half:]
        return jnp.concatenate([t1 * cos - t2 * sin, t2 * cos + t1 * sin], axis=-1)

    q = norm_rot(wq_ref, qg_ref[...]).astype(BF16)
    k = norm_rot(wk_ref, kg_ref[...]) * (RET_QK_DIM ** -0.5)
    v = jnp.dot(h, wv_ref[...], preferred_element_type=F32).astype(BF16)

    din = din_ref[0]
    dq = dq_ref[0]
    dk = dk_ref[0]
    dchunk = dq[c - 1:c, :]
    chunks = [slice(ci * c, (ci + 1) * c) for ci in range(q.shape[0] // c)]

    inner = [lax.dot_general(q[sl], k[sl].astype(BF16), (((1,), (1,)), ((), ())),
                             preferred_element_type=F32) * din for sl in chunks]
    kv = [lax.dot_general((k[sl] * dk).astype(BF16), v[sl], (((0,), (0,)), ((), ())),
                          preferred_element_type=F32) for sl in chunks]
    states = [state_ref[hd]]
    for ci in range(len(chunks)):
        states.append(states[ci] * dchunk + kv[ci])
    state_ref[hd] = states[-1]
    outs = []
    for ci, sl in enumerate(chunks):
        o = jnp.dot(inner[ci].astype(BF16), v[sl], preferred_element_type=F32)
        outs.append(o + jnp.dot(q[sl], states[ci].astype(BF16), preferred_element_type=F32) * dq)
    g = jnp.dot(h, wg_ref[...], preferred_element_type=F32)

    ys = []
    for ci, sl in enumerate(chunks):
        o = outs[ci]
        mu = jnp.mean(o, axis=-1, keepdims=True)
        var = jnp.mean(jnp.square(o - mu), axis=-1, keepdims=True)
        on = (o - mu) * lax.rsqrt(var + GN_EPS)
        on = on * gw_ref[...] + gb_ref[...]
        ys.append((_silu(g[sl]) * on).astype(BF16))
    y = jnp.concatenate(ys, axis=0)
    o_ref[...] += jnp.dot(y, wo_ref[...], preferred_element_type=F32)


def _ret_tables(s, heads):
    c = RET_CHUNK
    half = RET_QK_DIM // 2
    inv_freq = ROPE_BASE ** (-jnp.arange(half, dtype=F32) / half)
    ang = jnp.arange(s).astype(F32)[:, None] * inv_freq[None, :]
    log_g = jnp.log(1.0 - jnp.exp2(-5.0 - jnp.arange(heads, dtype=F32)))
    idx = jnp.arange(c, dtype=F32)
    diff = idx[:, None] - idx[None, :]
    din = jnp.where(diff >= 0, jnp.exp(log_g[:, None, None] * jnp.maximum(diff, 0.0)), 0.0)
    dq = jnp.exp(log_g[:, None] * (idx + 1.0))[..., None]
    dk = jnp.exp(log_g[:, None] * (c - 1.0 - idx))[..., None]
    return jnp.cos(ang), jnp.sin(ang), din, dq, dk


def ret_layer(x2, s, gain, w_in, q_gain, k_gain, gn_w, gn_b, w_out, *, tm):
    t, d = x2.shape
    c = RET_CHUNK
    dqk, dv = RET_QK_DIM, RET_V_DIM
    heads = w_out.shape[0] // dv
    tm = min(tm, s)
    tps = s // tm
    cos, sin, din, dq, dk = _ret_tables(s, heads)
    v_off = 2 * heads * dqk // dv
    g_off = v_off + heads
    return pl.pallas_call(
        functools.partial(_ret_layer_kernel, tiles_per_seq=tps),
        grid=(t // tm, heads),
        in_specs=[
            pl.BlockSpec((tm, d), lambda i, h: (i, 0)),
            pl.BlockSpec((1, d), lambda i, h: (0, 0)),
            pl.BlockSpec((d, dqk), lambda i, h: (0, h)),
            pl.BlockSpec((d, dqk), lambda i, h: (0, heads + h)),
            pl.BlockSpec((d, dv), lambda i, h: (0, v_off + h)),
            pl.BlockSpec((d, dv), lambda i, h: (0, g_off + h)),
            pl.BlockSpec((1, dqk), lambda i, h: (0, 0)),
            pl.BlockSpec((1, dqk), lambda i, h: (0, 0)),
            pl.BlockSpec((tm, dqk // 2), lambda i, h: (i % tps, 0)),
            pl.BlockSpec((tm, dqk // 2), lambda i, h: (i % tps, 0)),
            pl.BlockSpec((1, c, c), lambda i, h: (h, 0, 0)),
            pl.BlockSpec((1, c, 1), lambda i, h: (h, 0, 0)),
            pl.BlockSpec((1, c, 1), lambda i, h: (h, 0, 0)),
            pl.BlockSpec((1, dv), lambda i, h: (0, h)),
            pl.BlockSpec((1, dv), lambda i, h: (0, h)),
            pl.BlockSpec((dv, d), lambda i, h: (h, 0)),
        ],
        out_specs=pl.BlockSpec((tm, d), lambda i, h: (i, 0)),
        out_shape=jax.ShapeDtypeStruct((t, d), F32),
        scratch_shapes=[pltpu.VMEM((tm, d), BF16), pltpu.VMEM((heads, dqk, dv), F32)],
        compiler_params=_params("arbitrary", "arbitrary"),
        name="ret_layer",
    )(x2, gain.reshape(1, d), w_in, w_in, w_in, w_in, q_gain.reshape(1, dqk), k_gain.reshape(1, dqk),
      cos, sin, din, dq, dk, gn_w.reshape(1, heads * dv), gn_b.reshape(1, heads * dv), w_out)


def _sb_layer_kernel(x_ref, gain_ref, wq_ref, wk_ref, wv_ref, wg_ref, qg_ref, kg_ref, u_ref, wo_ref, o_ref,
                     h_ref, kc0_ref, kc1_ref, vc0_ref, vc1_ref, qs0_ref, qs1_ref, gs0_ref, gs1_ref,
                     acc_ref, run_ref, *, tb, hp, n_pairs, tiles_per_seq):
    i, p = pl.program_id(0), pl.program_id(1)
    d = SB_HEAD_DIM
    tm = x_ref.shape[0]
    nq = tm // tb
    row0 = pl.multiple_of((i % tiles_per_seq) * tm, tm)
    kc, vc, qs, gs = (kc0_ref, kc1_ref), (vc0_ref, vc1_ref), (qs0_ref, qs1_ref), (gs0_ref, gs1_ref)
    row = lax.broadcasted_iota(jnp.int32, (tb, tb), 0)
    col = lax.broadcasted_iota(jnp.int32, (tb, tb), 1)
    causal = col < row

    def rms(t, gain):
        return t * lax.rsqrt(jnp.mean(t * t, axis=-1, keepdims=True) + EPS) * gain

    def project_qk(par, slot):
        h = h_ref[...]
        q = jnp.dot(h, wq_ref[...], preferred_element_type=F32)
        k = jnp.dot(h, wk_ref[...], preferred_element_type=F32)
        for hh in range(hp):
            cs = slice(hh * d, (hh + 1) * d)
            qs[par][hh] = (rms(q[:, cs], qg_ref[...]) * (d ** -0.5)).astype(BF16)
            kc[par][slot * hp + hh, pl.ds(row0, tm), :] = rms(k[:, cs], kg_ref[...]).astype(BF16)

    def project_vg(par, slot):
        h = h_ref[...]
        v = jnp.dot(h, wv_ref[...], preferred_element_type=F32)
        gs[par][...] = jnp.dot(h, wg_ref[...], preferred_element_type=F32)
        for hh in range(hp):
            vc[par][slot * hp + hh, pl.ds(row0, tm), :] = v[:, hh * d:(hh + 1) * d].astype(BF16)

    def softplus(z):
        return jnp.maximum(z, 0.0) + jnp.log(1.0 + jnp.exp(-jnp.abs(z)))

    def suffix(sp):
        hi = sp.astype(BF16)
        lo = (sp - hi.astype(F32)).astype(BF16)
        return jnp.dot(jnp.concatenate([hi, lo], axis=1), u_ref[...], preferred_element_type=F32)

    def scores(qi, par, head, j0):
        kj = kc[par][head, pl.ds(j0, tb), :]
        return lax.dot_general(qi, kj, (((1,), (1,)), ((), ())), preferred_element_type=F32)

    def off_tile(qi, par, head, j0, acc, run):
        z = scores(qi, par, head, j0)
        sp = softplus(z)
        suf = suffix(sp)
        w = jnp.exp(z - sp - (suf + run))
        acc = acc + jnp.dot(w.astype(BF16), vc[par][head, pl.ds(j0, tb), :], preferred_element_type=F32)
        return acc, run + suf[:, 0:1] + sp[:, 0:1]

    def attend_first(par, slot, between):
        chains = [(qt, hh) for qt in range(nq) for hh in range(hp)]
        r0 = [pl.multiple_of(row0 + qt * tb, tb) for qt in range(nq)]
        j0 = [pl.multiple_of(jnp.maximum(r - tb, 0), tb) for r in r0]
        zd, zo, spd, spo, sufd, sufo = {}, {}, {}, {}, {}, {}
        for c in chains:
            qt, hh = c
            qi = qs[par][hh, qt * tb:(qt + 1) * tb, :]
            zd[c] = scores(qi, par, slot * hp + hh, r0[qt])
            zo[c] = scores(qi, par, slot * hp + hh, j0[qt])
        between[0]()
        for c in chains:
            spd[c] = jnp.where(causal, softplus(zd[c]), 0.0)
            spo[c] = softplus(zo[c])
            sufd[c] = suffix(spd[c])
            sufo[c] = suffix(spo[c])
        between[1]()
        rmin = []
        for c in chains:
            qt, hh = c
            head = slot * hp + hh
            run = sufd[c][:, 0:1] + spd[c][:, 0:1]
            wd = jnp.where(causal, jnp.exp(zd[c] - spd[c] - sufd[c]), 0.0)
            acc = jnp.dot(wd.astype(BF16), vc[par][head, pl.ds(r0[qt], tb), :], preferred_element_type=F32)
            wo = jnp.exp(zo[c] - spo[c] - (sufo[c] + run))
            acc2 = acc + jnp.dot(wo.astype(BF16), vc[par][head, pl.ds(j0[qt], tb), :],
                                 preferred_element_type=F32)
            run2 = run + sufo[c][:, 0:1] + spo[c][:, 0:1]
            if qt == 0:
                acc2 = jnp.where(r0[qt] > 0, acc2, acc)
                run2 = jnp.where(r0[qt] > 0, run2, run)
            acc_ref[qt, hh] = acc2
            run_ref[qt, hh] = run2
            rmin.append(jnp.min(run2))
        return rmin

    def attend_rest(par, slot, rmin):
        for qt in range(nq):
            r0 = pl.multiple_of(row0 + qt * tb, tb)
            for hh in range(hp):
                def more(c):
                    j, r = c
                    return jnp.logical_and(j >= 0, r <= SB_UNDERFLOW)

                def ktile(c, qt=qt, hh=hh):
                    j, _ = c
                    qi = qs[par][hh, qt * tb:(qt + 1) * tb, :]
                    a, r = off_tile(qi, par, slot * hp + hh, pl.multiple_of(j * tb, tb),
                                    acc_ref[qt, hh], run_ref[qt, hh])
                    acc_ref[qt, hh] = a
                    run_ref[qt, hh] = r
                    return j - 1, jnp.min(r)

                lax.while_loop(more, ktile, (r0 // tb - 2, rmin[qt * hp + hh]))

    def finish(par):
        g = gs[par][...]
        y = jnp.concatenate(
            [jnp.concatenate([acc_ref[qt, hh] for hh in range(hp)], axis=1) for qt in range(nq)], axis=0)
        y = (_silu(g) * y).astype(BF16)
        o_ref[...] += jnp.dot(y, wo_ref[...], preferred_element_type=F32)

    def step(attend_par, project_par):
        if project_par is None:
            between = (lambda: None, lambda: None)
        else:
            between = (functools.partial(project_qk, project_par, p // 2),
                       functools.partial(project_vg, project_par, p // 2))
        if attend_par is None:
            between[0]()
            between[1]()
        else:
            a_slot = (p - 1) // 2
            rmin = attend_first(attend_par, a_slot, between)
            attend_rest(attend_par, a_slot, rmin)
            finish(attend_par)

    last_par = (n_pairs - 1) % 2

    @pl.when(p == 0)
    def _():
        _rmsnorm_rows(x_ref, gain_ref, h_ref, 128, copy_ref=o_ref)
        step(None, 0)

    @pl.when(jnp.logical_and(jnp.logical_and(p > 0, p < n_pairs), p % 2 == 1))
    def _():
        step(0, 1)

    @pl.when(jnp.logical_and(jnp.logical_and(p > 0, p < n_pairs), p % 2 == 0))
    def _():
        step(1, 0)

    @pl.when(p == n_pairs)
    def _():
        step(last_par, None)


def sb_layer(x2, s, gain, w_in, q_gain, k_gain, w_out, *, tm, tb, hp):
    t, dm = x2.shape
    d = SB_HEAD_DIM
    width = w_out.shape[0]
    heads = width // d
    tm, tb = min(tm, s), min(tb, s)
    n_pairs = heads // hp
    half = (n_pairs + 1) // 2
    u = (jnp.arange(tb)[:, None] > jnp.arange(tb)[None, :]).astype(BF16)
    u = jnp.concatenate([u, u], axis=0)

    def stream(g):
        return pl.BlockSpec((dm, hp * d), lambda i, p: (0, g * n_pairs + jnp.minimum(p, n_pairs - 1)))

    cache = pltpu.VMEM((half * hp, s, d), BF16)
    return pl.pallas_call(
        functools.partial(_sb_layer_kernel, tb=tb, hp=hp, n_pairs=n_pairs, tiles_per_seq=s // tm),
        grid=(t // tm, n_pairs + 1),
        in_specs=[pl.BlockSpec((tm, dm), lambda i, p: (i, 0)),
                  pl.BlockSpec((1, dm), lambda i, p: (0, 0)),
                  stream(0), stream(1), stream(2), stream(3),
                  pl.BlockSpec((1, d), lambda i, p: (0, 0)),
                  pl.BlockSpec((1, d), lambda i, p: (0, 0)),
                  pl.BlockSpec((2 * tb, tb), lambda i, p: (0, 0)),
                  pl.BlockSpec((hp * d, dm), lambda i, p: (jnp.maximum(p - 1, 0), 0))],
        out_specs=pl.BlockSpec((tm, dm), lambda i, p: (i, 0)),
        out_shape=jax.ShapeDtypeStruct((t, dm), F32),
        scratch_shapes=[pltpu.VMEM((tm, dm), BF16), cache, cache, cache, cache,
                        pltpu.VMEM((hp, tm, d), BF16), pltpu.VMEM((hp, tm, d), BF16),
                        pltpu.VMEM((tm, hp * d), F32), pltpu.VMEM((tm, hp * d), F32),
                        pltpu.VMEM((tm // tb, hp, tb, d), F32), pltpu.VMEM((tm // tb, hp, tb, 1), F32)],
        compiler_params=_params("arbitrary", "arbitrary"),
        name="sb_layer",
    )(x2, gain.reshape(1, dm), w_in, w_in, w_in, w_in, q_gain.reshape(1, d), k_gain.reshape(1, d), u, w_out)


def kernel(x, conv_norm, conv_w_in, conv_w, conv_b, conv_w_out, ret_norm, ret_w_in, ret_q_gain, ret_k_gain,
           ret_gn_w, ret_gn_b, ret_w_out, sb_norm, sb_w_in, sb_q_gain, sb_k_gain, sb_w_out):
    bsz, s, d = x.shape
    depth = conv_norm.shape[0] + ret_norm.shape[0] + sb_norm.shape[0]
    x2 = x.reshape(bsz * s, d)
    for i in range(depth):
        kind, j = i % N_MIXERS, i // N_MIXERS
        if kind == 0:
            x2 = conv_layer(x2, s, conv_norm[j], conv_w_in[j].astype(BF16), conv_w[j], conv_b[j],
                            conv_w_out[j].astype(BF16), tm=512, ec=512)
        elif kind == 1:
            x2 = ret_layer(x2, s, ret_norm[j], ret_w_in[j].astype(BF16), ret_q_gain[j], ret_k_gain[j],
                           ret_gn_w[j], ret_gn_b[j], ret_w_out[j].astype(BF16), tm=512)
        else:
            x2 = sb_layer(x2, s, sb_norm[j], sb_w_in[j].astype(BF16), sb_q_gain[j], sb_k_gain[j],
                          sb_w_out[j].astype(BF16), tm=512, tb=256, hp=2)
    return x2.reshape(bsz, s, d)
```

```python
import functools

import jax
import jax.numpy as jnp
from jax import lax
from jax.experimental import pallas as pl
from jax.experimental.pallas import tpu as pltpu

EPS = 1e-6
GN_EPS = 1e-5
N_MIXERS = 3
CONV_K = 3
RET_QK_DIM = 256
RET_V_DIM = 512
RET_CHUNK = 256
ROPE_BASE = 10000.0
SB_HEAD_DIM = 128
SB_UNDERFLOW = 104.0

F32 = jnp.float32
BF16 = jnp.bfloat16

VMEM_LIMIT_BYTES = 56 * 1024 * 1024


def _params(*sem):
    return pltpu.CompilerParams(dimension_semantics=sem, vmem_limit_bytes=VMEM_LIMIT_BYTES)


def _silu(g):
    return g * jax.nn.sigmoid(g)


def _rmsnorm_rows(x_ref, g_ref, h_ref, rows, copy_ref=None):
    def body(r, c):
        sl = pl.ds(pl.multiple_of(r * rows, rows), rows)
        x = x_ref[sl, :]
        ms = jnp.mean(x * x, axis=-1, keepdims=True)
        h_ref[sl, :] = (x * lax.rsqrt(ms + EPS) * g_ref[...]).astype(h_ref.dtype)
        if copy_ref is not None:
            copy_ref[sl, :] = x
        return c

    lax.fori_loop(0, x_ref.shape[0] // rows, body, 0)


def _conv_layer_kernel(x_ref, gain_ref, wb_ref, wc_ref, wu_ref, wg_ref, cw_ref, cb_ref, wo_ref, o_ref,
                       h_ref, carry_ref, *, tiles_per_seq):
    i, j = pl.program_id(0), pl.program_id(1)

    @pl.when(j == 0)
    def _():
        _rmsnorm_rows(x_ref, gain_ref, h_ref, 128, copy_ref=o_ref)

    h = h_ref[...]
    v = (jnp.dot(h, wc_ref[...], preferred_element_type=F32)
         * jnp.dot(h, wu_ref[...], preferred_element_type=F32))
    gate = _silu(jnp.dot(h, wg_ref[...], preferred_element_type=F32))
    tm = v.shape[0]
    prev = jnp.where(i % tiles_per_seq == 0, 0.0, carry_ref[j])
    carry_ref[j] = v[tm - 8:, :]
    p1, p2 = prev[7:8, :], prev[6:7, :]
    row = lax.broadcasted_iota(jnp.int32, v.shape, 0)
    v1 = jnp.where(row >= 1, pltpu.roll(v, 1, 0), p1)
    v2 = jnp.where(row >= 2, pltpu.roll(v, 2, 0), jnp.where(row == 1, p1, p2))
    conv = cb_ref[...] + v2 * cw_ref[0:1, :]
    conv = conv + v1 * cw_ref[1:2, :]
    conv = conv + v * cw_ref[2:3, :]
    y = jnp.dot(h, wb_ref[...], preferred_element_type=F32) * conv
    y = y * gate
    o_ref[...] += jnp.dot(y.astype(BF16), wo_ref[...], preferred_element_type=F32)


def conv_layer(x2, s, gain, w_in, conv_w, conv_b, w_out, *, tm, ec):
    t, d = x2.shape
    e = w_out.shape[0]
    tm, ec = min(tm, s), min(ec, e)
    nj = e // ec

    def stream(g):
        return pl.BlockSpec((d, ec), lambda i, j: (0, g * nj + j))

    return pl.pallas_call(
        functools.partial(_conv_layer_kernel, tiles_per_seq=s // tm),
        grid=(t // tm, nj),
        in_specs=[pl.BlockSpec((tm, d), lambda i, j: (i, 0)),
                  pl.BlockSpec((1, d), lambda i, j: (0, 0)),
                  stream(0), stream(1), stream(2), stream(3),
                  pl.BlockSpec((CONV_K, ec), lambda i, j: (0, j)),
                  pl.BlockSpec((1, ec), lambda i, j: (0, j)),
                  pl.BlockSpec((ec, d), lambda i, j: (j, 0))],
        out_specs=pl.BlockSpec((tm, d), lambda i, j: (i, 0)),
        out_shape=jax.ShapeDtypeStruct((t, d), F32),
        scratch_shapes=[pltpu.VMEM((tm, d), BF16), pltpu.VMEM((nj, 8, ec), F32)],
        compiler_params=_params("arbitrary", "arbitrary"),
        name="conv_layer",
    )(x2, gain.reshape(1, d), w_in, w_in, w_in, w_in, conv_w, conv_b.reshape(1, e), w_out)


def _ret_layer_kernel(x_ref, gain_ref, wq_ref, wk_ref, wv_ref, wg_ref, qg_ref, kg_ref, cos_ref, sin_ref,
                      din_ref, dq_ref, dk_ref, gw_ref, gb_ref, wo_ref, o_ref, h_ref, state_ref, *, tiles_per_seq):
    i, hd = pl.program_id(0), pl.program_id(1)
    c = RET_CHUNK
    half = RET_QK_DIM // 2

    @pl.when(hd == 0)
    def _():
        _rmsnorm_rows(x_ref, gain_ref, h_ref, 128, copy_ref=o_ref)

    @pl.when(i % tiles_per_seq == 0)
    def _():
        state_ref[hd] = jnp.zeros(state_ref.shape[1:], F32)

    h = h_ref[...]
    cos, sin = cos_ref[...], sin_ref[...]

    def norm_rot(w_ref, gain):
        t = jnp.dot(h, w_ref[...], preferred_element_type=F32)
        t = t * lax.rsqrt(jnp.mean(t * t, axis=-1, keepdims=True) + EPS) * gain
        t1, t2 = t[:, :half], t[:, half:]
        return jnp.concatenate([t1 * cos - t2 * sin, t2 * cos + t1 * sin], axis=-1)

    q = norm_rot(wq_ref, qg_ref[...]).astype(BF16)
    k = norm_rot(wk_ref, kg_ref[...]) * (RET_QK_DIM ** -0.5)
    v = jnp.dot(h, wv_ref[...], preferred_element_type=F32).astype(BF16)

    din = din_ref[0]
    dq = dq_ref[0]
    dk = dk_ref[0]
    dchunk = dq[c - 1:c, :]
    chunks = [slice(ci * c, (ci + 1) * c) for ci in range(q.shape[0] // c)]

    inner = [lax.dot_general(q[sl], k[sl].astype(BF16), (((1,), (1,)), ((), ())),
                             preferred_element_type=F32) * din for sl in chunks]
    kv = [lax.dot_general((k[sl] * dk).astype(BF16), v[sl], (((0,), (0,)), ((), ())),
                          preferred_element_type=F32) for sl in chunks]
    states = [state_ref[hd]]
    for ci in range(len(chunks)):
        states.append(states[ci] * dchunk + kv[ci])
    state_ref[hd] = states[-1]
    outs = []
    for ci, sl in enumerate(chunks):
        o = jnp.dot(inner[ci].astype(BF16), v[sl], preferred_element_type=F32)
        outs.append(o + jnp.dot(q[sl], states[ci].astype(BF16), preferred_element_type=F32) * dq)
    g = jnp.dot(h, wg_ref[...], preferred_element_type=F32)

    ys = []
    for ci, sl in enumerate(chunks):
        o = outs[ci]
        mu = jnp.mean(o, axis=-1, keepdims=True)
        var = jnp.mean(jnp.square(o - mu), axis=-1, keepdims=True)
        on = (o - mu) * lax.rsqrt(var + GN_EPS)
        on = on * gw_ref[...] + gb_ref[...]
        ys.append((_silu(g[sl]) * on).astype(BF16))
    y = jnp.concatenate(ys, axis=0)
    o_ref[...] += jnp.dot(y, wo_ref[...], preferred_element_type=F32)


def _ret_tables(s, heads):
    c = RET_CHUNK
    half = RET_QK_DIM // 2
    inv_freq = ROPE_BASE ** (-jnp.arange(half, dtype=F32) / half)
    ang = jnp.arange(s).astype(F32)[:, None] * inv_freq[None, :]
    log_g = jnp.log(1.0 - jnp.exp2(-5.0 - jnp.arange(heads, dtype=F32)))
    idx = jnp.arange(c, dtype=F32)
    diff = idx[:, None] - idx[None, :]
    din = jnp.where(diff >= 0, jnp.exp(log_g[:, None, None] * jnp.maximum(diff, 0.0)), 0.0)
    dq = jnp.exp(log_g[:, None] * (idx + 1.0))[..., None]
    dk = jnp.exp(log_g[:, None] * (c - 1.0 - idx))[..., None]
    return jnp.cos(ang), jnp.sin(ang), din, dq, dk


def ret_layer(x2, s, gain, w_in, q_gain, k_gain, gn_w, gn_b, w_out, *, tm):
    t, d = x2.shape
    c = RET_CHUNK
    dqk, dv = RET_QK_DIM, RET_V_DIM
    heads = w_out.shape[0] // dv
    tm = min(tm, s)
    tps = s // tm
    cos, sin, din, dq, dk = _ret_tables(s, heads)
    v_off = 2 * heads * dqk // dv
    g_off = v_off + heads
    return pl.pallas_call(
        functools.partial(_ret_layer_kernel, tiles_per_seq=tps),
        grid=(t // tm, heads),
        in_specs=[
            pl.BlockSpec((tm, d), lambda i, h: (i, 0)),
            pl.BlockSpec((1, d), lambda i, h: (0, 0)),
            pl.BlockSpec((d, dqk), lambda i, h: (0, h)),
            pl.BlockSpec((d, dqk), lambda i, h: (0, heads + h)),
            pl.BlockSpec((d, dv), lambda i, h: (0, v_off + h)),
            pl.BlockSpec((d, dv), lambda i, h: (0, g_off + h)),
            pl.BlockSpec((1, dqk), lambda i, h: (0, 0)),
            pl.BlockSpec((1, dqk), lambda i, h: (0, 0)),
            pl.BlockSpec((tm, dqk // 2), lambda i, h: (i % tps, 0)),
            pl.BlockSpec((tm, dqk // 2), lambda i, h: (i % tps, 0)),
            pl.BlockSpec((1, c, c), lambda i, h: (h, 0, 0)),
            pl.BlockSpec((1, c, 1), lambda i, h: (h, 0, 0)),
            pl.BlockSpec((1, c, 1), lambda i, h: (h, 0, 0)),
            pl.BlockSpec((1, dv), lambda i, h: (0, h)),
            pl.BlockSpec((1, dv), lambda i, h: (0, h)),
            pl.BlockSpec((dv, d), lambda i, h: (h, 0)),
        ],
        out_specs=pl.BlockSpec((tm, d), lambda i, h: (i, 0)),
        out_shape=jax.ShapeDtypeStruct((t, d), F32),
        scratch_shapes=[pltpu.VMEM((tm, d), BF16), pltpu.VMEM((heads, dqk, dv), F32)],
        compiler_params=_params("arbitrary", "arbitrary"),
        name="ret_layer",
    )(x2, gain.reshape(1, d), w_in, w_in, w_in, w_in, q_gain.reshape(1, dqk), k_gain.reshape(1, dqk),
      cos, sin, din, dq, dk, gn_w.reshape(1, heads * dv), gn_b.reshape(1, heads * dv), w_out)


def _sb_layer_kernel(x_ref, gain_ref, wq_ref, wk_ref, wv_ref, wg_ref, qg_ref, kg_ref, u_ref, wo_ref, o_ref,
                     h_ref, kc0_ref, kc1_ref, vc0_ref, vc1_ref, qs0_ref, qs1_ref, gs0_ref, gs1_ref,
                     acc_ref, run_ref, *, tb, hp, n_pairs, tiles_per_seq):
    i, p = pl.program_id(0), pl.program_id(1)
    d = SB_HEAD_DIM
    tm = x_ref.shape[0]
    nq = tm // tb
    row0 = pl.multiple_of((i % tiles_per_seq) * tm, tm)
    kc, vc, qs, gs = (kc0_ref, kc1_ref), (vc0_ref, vc1_ref), (qs0_ref, qs1_ref), (gs0_ref, gs1_ref)
    row = lax.broadcasted_iota(jnp.int32, (tb, tb), 0)
    col = lax.broadcasted_iota(jnp.int32, (tb, tb), 1)
    causal = col < row

    def rms(t, gain):
        return t * lax.rsqrt(jnp.mean(t * t, axis=-1, keepdims=True) + EPS) * gain

    def project_qk(par, slot):
        h = h_ref[...]
        q = jnp.dot(h, wq_ref[...], preferred_element_type=F32)
        k = jnp.dot(h, wk_ref[...], preferred_element_type=F32)
        for hh in range(hp):
            cs = slice(hh * d, (hh + 1) * d)
            qs[par][hh] = (rms(q[:, cs], qg_ref[...]) * (d ** -0.5)).astype(BF16)
            kc[par][slot * hp + hh, pl.ds(row0, tm), :] = rms(k[:, cs], kg_ref[...]).astype(BF16)

    def project_vg(par, slot):
        h = h_ref[...]
        v = jnp.dot(h, wv_ref[...], preferred_element_type=F32)
        gs[par][...] = jnp.dot(h, wg_ref[...], preferred_element_type=F32)
        for hh in range(hp):
            vc[par][slot * hp + hh, pl.ds(row0, tm), :] = v[:, hh * d:(hh + 1) * d].astype(BF16)

    def softplus(z):
        return jnp.maximum(z, 0.0) + jnp.log(1.0 + jnp.exp(-jnp.abs(z)))

    def split(sp):
        hi = sp.astype(BF16)
        lo = (sp - hi.astype(F32)).astype(BF16)
        return jnp.concatenate([hi, lo], axis=1)

    def suffix_many(sps):
        suf = jnp.dot(jnp.concatenate([split(sp) for sp in sps], axis=0), u_ref[...],
                      preferred_element_type=F32)
        return [suf[n * tb:(n + 1) * tb, :] for n in range(len(sps))]

    def suffix(sp):
        return suffix_many([sp])[0]

    def scores(qi, par, head, j0):
        kj = kc[par][head, pl.ds(j0, tb), :]
        return lax.dot_general(qi, kj, (((1,), (1,)), ((), ())), preferred_element_type=F32)

    def off_tile(qi, par, head, j0, acc, run):
        z = scores(qi, par, head, j0)
        sp = softplus(z)
        suf = suffix(sp)
        w = jnp.exp(z - sp - (suf + run))
        acc = acc + jnp.dot(w.astype(BF16), vc[par][head, pl.ds(j0, tb), :], preferred_element_type=F32)
        return acc, run + suf[:, 0:1] + sp[:, 0:1]

    def attend_first(par, slot, between):
        chains = [(qt, hh) for qt in range(nq) for hh in range(hp)]
        r0 = [pl.multiple_of(row0 + qt * tb, tb) for qt in range(nq)]
        j0 = [pl.multiple_of(jnp.maximum(r - tb, 0), tb) for r in r0]
        zd, zo, spd, spo, sufd, sufo = {}, {}, {}, {}, {}, {}
        for c in chains:
            qt, hh = c
            qi = qs[par][hh, qt * tb:(qt + 1) * tb, :]
            zd[c] = scores(qi, par, slot * hp + hh, r0[qt])
            zo[c] = scores(qi, par, slot * hp + hh, j0[qt])
        between[0]()
        for c in chains:
            spd[c] = jnp.where(causal, softplus(zd[c]), 0.0)
            spo[c] = softplus(zo[c])
        sufs = suffix_many([spd[c] for c in chains] + [spo[c] for c in chains])
        for n, c in enumerate(chains):
            sufd[c], sufo[c] = sufs[n], sufs[len(chains) + n]
        between[1]()
        rmin = []
        for c in chains:
            qt, hh = c
            head = slot * hp + hh
            run = sufd[c][:, 0:1] + spd[c][:, 0:1]
            wd = jnp.where(causal, jnp.exp(zd[c] - spd[c] - sufd[c]), 0.0)
            acc = jnp.dot(wd.astype(BF16), vc[par][head, pl.ds(r0[qt], tb), :], preferred_element_type=F32)
            wo = jnp.exp(zo[c] - spo[c] - (sufo[c] + run))
            acc2 = acc + jnp.dot(wo.astype(BF16), vc[par][head, pl.ds(j0[qt], tb), :],
                                 preferred_element_type=F32)
            run2 = run + sufo[c][:, 0:1] + spo[c][:, 0:1]
            if qt == 0:
                acc2 = jnp.where(r0[qt] > 0, acc2, acc)
                run2 = jnp.where(r0[qt] > 0, run2, run)
            acc_ref[qt, hh] = acc2
            run_ref[qt, hh] = run2
            rmin.append(run2)
        return jnp.min(functools.reduce(jnp.minimum, rmin))

    def attend_rest(par, slot, rmin_all):
        @pl.when(rmin_all <= SB_UNDERFLOW)
        def _():
            for qt in range(nq):
                r0 = pl.multiple_of(row0 + qt * tb, tb)
                for hh in range(hp):
                    def more(c):
                        j, r = c
                        return jnp.logical_and(j >= 0, r <= SB_UNDERFLOW)

                    def ktile(c, qt=qt, hh=hh):
                        j, _ = c
                        qi = qs[par][hh, qt * tb:(qt + 1) * tb, :]
                        a, r = off_tile(qi, par, slot * hp + hh, pl.multiple_of(j * tb, tb),
                                        acc_ref[qt, hh], run_ref[qt, hh])
                        acc_ref[qt, hh] = a
                        run_ref[qt, hh] = r
                        return j - 1, jnp.min(r)

                    lax.while_loop(more, ktile, (r0 // tb - 2, jnp.min(run_ref[qt, hh])))

    def finish(par):
        g = gs[par][...]
        y = jnp.concatenate(
            [jnp.concatenate([acc_ref[qt, hh] for hh in range(hp)], axis=1) for qt in range(nq)], axis=0)
        y = (_silu(g) * y).astype(BF16)
        o_ref[...] += jnp.dot(y, wo_ref[...], preferred_element_type=F32)

    def step(attend_par, project_par):
        if project_par is None:
            between = (lambda: None, lambda: None)
        else:
            between = (functools.partial(project_qk, project_par, p // 2),
                       functools.partial(project_vg, project_par, p // 2))
        if attend_par is None:
            between[0]()
            between[1]()
        else:
            a_slot = (p - 1) // 2
            rmin = attend_first(attend_par, a_slot, between)
            attend_rest(attend_par, a_slot, rmin)
            finish(attend_par)

    last_par = (n_pairs - 1) % 2

    @pl.when(p == 0)
    def _():
        _rmsnorm_rows(x_ref, gain_ref, h_ref, 128, copy_ref=o_ref)
        step(None, 0)

    @pl.when(jnp.logical_and(jnp.logical_and(p > 0, p < n_pairs), p % 2 == 1))
    def _():
        step(0, 1)

    @pl.when(jnp.logical_and(jnp.logical_and(p > 0, p < n_pairs), p % 2 == 0))
    def _():
        step(1, 0)

    @pl.when(p == n_pairs)
    def _():
        step(last_par, None)


def sb_layer(x2, s, gain, w_in, q_gain, k_gain, w_out, *, tm, tb, hp):
    t, dm = x2.shape
    d = SB_HEAD_DIM
    width = w_out.shape[0]
    heads = width // d
    tm, tb = min(tm, s), min(tb, s)
    n_pairs = heads // hp
    half = (n_pairs + 1) // 2
    u = (jnp.arange(tb)[:, None] > jnp.arange(tb)[None, :]).astype(BF16)
    u = jnp.concatenate([u, u], axis=0)

    def stream(g):
        return pl.BlockSpec((dm, hp * d), lambda i, p: (0, g * n_pairs + jnp.minimum(p, n_pairs - 1)))

    cache = pltpu.VMEM((half * hp, s, d), BF16)
    return pl.pallas_call(
        functools.partial(_sb_layer_kernel, tb=tb, hp=hp, n_pairs=n_pairs, tiles_per_seq=s // tm),
        grid=(t // tm, n_pairs + 1),
        in_specs=[pl.BlockSpec((tm, dm), lambda i, p: (i, 0)),
                  pl.BlockSpec((1, dm), lambda i, p: (0, 0)),
                  stream(0), stream(1), stream(2), stream(3),
                  pl.BlockSpec((1, d), lambda i, p: (0, 0)),
                  pl.BlockSpec((1, d), lambda i, p: (0, 0)),
                  pl.BlockSpec((2 * tb, tb), lambda i, p: (0, 0)),
                  pl.BlockSpec((hp * d, dm), lambda i, p: (jnp.maximum(p - 1, 0), 0))],
        out_specs=pl.BlockSpec((tm, dm), lambda i, p: (i, 0)),
        out_shape=jax.ShapeDtypeStruct((t, dm), F32),
        scratch_shapes=[pltpu.VMEM((tm, dm), BF16), cache, cache, cache, cache,
                        pltpu.VMEM((hp, tm, d), BF16), pltpu.VMEM((hp, tm, d), BF16),
                        pltpu.VMEM((tm, hp * d), F32), pltpu.VMEM((tm, hp * d), F32),
                        pltpu.VMEM((tm // tb, hp, tb, d), F32), pltpu.VMEM((tm // tb, hp, tb, 1), F32)],
        compiler_params=_params("arbitrary", "arbitrary"),
        name="sb_layer",
    )(x2, gain.reshape(1, dm), w_in, w_in, w_in, w_in, q_gain.reshape(1, d), k_gain.reshape(1, d), u, w_out)


def kernel(x, conv_norm, conv_w_in, conv_w, conv_b, conv_w_out, ret_norm, ret_w_in, ret_q_gain, ret_k_gain,
           ret_gn_w, ret_gn_b, ret_w_out, sb_norm, sb_w_in, sb_q_gain, sb_k_gain, sb_w_out):
    bsz, s, d = x.shape
    depth = conv_norm.shape[0] + ret_norm.shape[0] + sb_norm.shape[0]
    x2 = x.reshape(bsz * s, d)
    for i in range(depth):
        kind, j = i % N_MIXERS, i // N_MIXERS
        if kind == 0:
            x2 = conv_layer(x2, s, conv_norm[j], conv_w_in[j].astype(BF16), conv_w[j], conv_b[j],
                            conv_w_out[j].astype(BF16), tm=512, ec=512)
        elif kind == 1:
            x2 = ret_layer(x2, s, ret_norm[j], ret_w_in[j].astype(BF16), ret_q_gain[j], ret_k_gain[j],
                           ret_gn_w[j], ret_gn_b[j], ret_w_out[j].astype(BF16), tm=512)
        else:
            x2 = sb_layer(x2, s, sb_norm[j], sb_w_in[j].astype(BF16), sb_q_gain[j], sb_k_gain[j],
                          sb_w_out[j].astype(BF16), tm=512, tb=256, hp=2)
    return x2.reshape(bsz, s, d)
```

```python
import functools

import jax
import jax.numpy as jnp
from jax import lax
from jax.experimental import pallas as pl
from jax.experimental.pallas import tpu as pltpu

EPS = 1e-6
GN_EPS = 1e-5
N_MIXERS = 3
CONV_K = 3
RET_QK_DIM = 256
RET_V_DIM = 512
RET_CHUNK = 128
ROPE_BASE = 10000.0
SB_HEAD_DIM = 128
SB_UNDERFLOW = 104.0

F32 = jnp.float32
BF16 = jnp.bfloat16

VMEM_LIMIT_BYTES = 56 * 1024 * 1024


def _params(*sem):
    return pltpu.CompilerParams(dimension_semantics=sem, vmem_limit_bytes=VMEM_LIMIT_BYTES)


def _silu(g):
    return g * jax.nn.sigmoid(g)


def _rmsnorm_rows(x_ref, g_ref, h_ref, rows, copy_ref=None):
    def body(r, c):
        sl = pl.ds(pl.multiple_of(r * rows, rows), rows)
        x = x_ref[sl, :]
        ms = jnp.mean(x * x, axis=-1, keepdims=True)
        h_ref[sl, :] = (x * lax.rsqrt(ms + EPS) * g_ref[...]).astype(h_ref.dtype)
        if copy_ref is not None:
            copy_ref[sl, :] = x
        return c

    lax.fori_loop(0, x_ref.shape[0] // rows, body, 0)


def _conv_layer_kernel(x_ref, xn_ref, gain_ref, wb_ref, wc_ref, wu_ref, wg_ref, cw_ref, cb_ref, wo_ref, o_ref,
                       h0_ref, h1_ref, carry_ref, *, tiles_per_seq):
    i, j = pl.program_id(0), pl.program_id(1)
    tm = x_ref.shape[0]
    rows = tm // pl.num_programs(1)

    @pl.when(jnp.logical_and(i == 0, j == 0))
    def _():
        _rmsnorm_rows(x_ref, gain_ref, h0_ref, 128)

    def step(h_ref, hn_ref, first):
        h = h_ref[...]
        v = (jnp.dot(h, wc_ref[...], preferred_element_type=F32)
             * jnp.dot(h, wu_ref[...], preferred_element_type=F32))
        gate = _silu(jnp.dot(h, wg_ref[...], preferred_element_type=F32))
        prev = jnp.where(i % tiles_per_seq == 0, 0.0, carry_ref[j])
        carry_ref[j] = v[tm - 8:, :]
        p1, p2 = prev[7:8, :], prev[6:7, :]
        row = lax.broadcasted_iota(jnp.int32, v.shape, 0)
        v1 = jnp.where(row >= 1, pltpu.roll(v, 1, 0), p1)
        v2 = jnp.where(row >= 2, pltpu.roll(v, 2, 0), jnp.where(row == 1, p1, p2))
        conv = cb_ref[...] + v2 * cw_ref[0:1, :]
        conv = conv + v1 * cw_ref[1:2, :]
        conv = conv + v * cw_ref[2:3, :]
        y = jnp.dot(h, wb_ref[...], preferred_element_type=F32) * conv
        y = y * gate
        contrib = jnp.dot(y.astype(BF16), wo_ref[...], preferred_element_type=F32)
        if first:
            o_ref[...] = x_ref[...] + contrib
        else:
            o_ref[...] += contrib
        sl = pl.ds(pl.multiple_of(j * rows, rows), rows)
        xn = xn_ref[sl, :]
        ms = jnp.mean(xn * xn, axis=-1, keepdims=True)
        hn_ref[sl, :] = (xn * lax.rsqrt(ms + EPS) * gain_ref[...]).astype(hn_ref.dtype)

    for parity, (h_ref, hn_ref) in enumerate(((h0_ref, h1_ref), (h1_ref, h0_ref))):
        for first in (True, False):
            pl.when(jnp.logical_and(i % 2 == parity, (j == 0) == first))(
                functools.partial(step, h_ref, hn_ref, first))


def conv_layer(x2, s, gain, w_in, conv_w, conv_b, w_out, *, tm, ec):
    t, d = x2.shape
    e = w_out.shape[0]
    tm, ec = min(tm, s), min(ec, e)
    nj = e // ec
    ni = t // tm
    assert tm % nj == 0 and (tm // nj) % 8 == 0

    def stream(g):
        return pl.BlockSpec((d, ec), lambda i, j: (0, g * nj + j))

    return pl.pallas_call(
        functools.partial(_conv_layer_kernel, tiles_per_seq=s // tm),
        grid=(ni, nj),
        in_specs=[pl.BlockSpec((tm, d), lambda i, j: (i, 0)),
                  pl.BlockSpec((tm, d), lambda i, j: (jnp.minimum(i + 1, ni - 1), 0)),
                  pl.BlockSpec((1, d), lambda i, j: (0, 0)),
                  stream(0), stream(1), stream(2), stream(3),
                  pl.BlockSpec((CONV_K, ec), lambda i, j: (0, j)),
                  pl.BlockSpec((1, ec), lambda i, j: (0, j)),
                  pl.BlockSpec((ec, d), lambda i, j: (j, 0))],
        out_specs=pl.BlockSpec((tm, d), lambda i, j: (i, 0)),
        out_shape=jax.ShapeDtypeStruct((t, d), F32),
        scratch_shapes=[pltpu.VMEM((tm, d), BF16), pltpu.VMEM((tm, d), BF16), pltpu.VMEM((nj, 8, ec), F32)],
        compiler_params=_params("arbitrary", "arbitrary"),
        name="conv_layer",
    )(x2, x2, gain.reshape(1, d), w_in, w_in, w_in, w_in, conv_w, conv_b.reshape(1, e), w_out)


def _ret_layer_kernel(x_ref, gain_ref, wq_ref, wk_ref, wv_ref, wg_ref, qg_ref, kg_ref, cos_ref, sin_ref,
                      din_ref, dq_ref, dk_ref, gw_ref, gb_ref, wo_ref, o_ref, h_ref, state_ref, *, tiles_per_seq):
    i, hd = pl.program_id(0), pl.program_id(1)
    c = RET_CHUNK
    half = RET_QK_DIM // 2

    @pl.when(hd == 0)
    def _():
        _rmsnorm_rows(x_ref, gain_ref, h_ref, 128, copy_ref=o_ref)

    @pl.when(i % tiles_per_seq == 0)
    def _():
        state_ref[hd] = jnp.zeros(state_ref.shape[1:], F32)

    h = h_ref[...]
    cos, sin = cos_ref[...], sin_ref[...]

    def norm_rot(w_ref, gain):
        t = jnp.dot(h, w_ref[...], preferred_element_type=F32)
        t = t * lax.rsqrt(jnp.mean(t * t, axis=-1, keepdims=True) + EPS) * gain
        t1, t2 = t[:, :half], t[:, half:]
        return jnp.concatenate([t1 * cos - t2 * sin, t2 * cos + t1 * sin], axis=-1)

    q = norm_rot(wq_ref, qg_ref[...]).astype(BF16)
    k = norm_rot(wk_ref, kg_ref[...]) * (RET_QK_DIM ** -0.5)
    v = jnp.dot(h, wv_ref[...], preferred_element_type=F32).astype(BF16)

    din = din_ref[0]
    dq = dq_ref[0]
    dk = dk_ref[0]
    dchunk = dq[c - 1:c, :]
    chunks = [slice(ci * c, (ci + 1) * c) for ci in range(q.shape[0] // c)]

    inner = [lax.dot_general(q[sl], k[sl].astype(BF16), (((1,), (1,)), ((), ())),
                             preferred_element_type=F32) * din for sl in chunks]
    kv = [lax.dot_general((k[sl] * dk).astype(BF16), v[sl], (((0,), (0,)), ((), ())),
                          preferred_element_type=F32) for sl in chunks]
    states = [state_ref[hd]]
    for ci in range(len(chunks)):
        states.append(states[ci] * dchunk + kv[ci])
    state_ref[hd] = states[-1]
    outs = []
    for ci, sl in enumerate(chunks):
        o = jnp.dot(inner[ci].astype(BF16), v[sl], preferred_element_type=F32)
        outs.append(o + jnp.dot(q[sl], states[ci].astype(BF16), preferred_element_type=F32) * dq)
    g = jnp.dot(h, wg_ref[...], preferred_element_type=F32)

    ys = []
    for ci, sl in enumerate(chunks):
        o = outs[ci]
        mu = jnp.mean(o, axis=-1, keepdims=True)
        var = jnp.mean(jnp.square(o - mu), axis=-1, keepdims=True)
        on = (o - mu) * lax.rsqrt(var + GN_EPS)
        on = on * gw_ref[...] + gb_ref[...]
        ys.append((_silu(g[sl]) * on).astype(BF16))
    y = jnp.concatenate(ys, axis=0)
    o_ref[...] += jnp.dot(y, wo_ref[...], preferred_element_type=F32)


def _ret_tables(s, heads):
    c = RET_CHUNK
    half = RET_QK_DIM // 2
    inv_freq = ROPE_BASE ** (-jnp.arange(half, dtype=F32) / half)
    ang = jnp.arange(s).astype(F32)[:, None] * inv_freq[None, :]
    log_g = jnp.log(1.0 - jnp.exp2(-5.0 - jnp.arange(heads, dtype=F32)))
    idx = jnp.arange(c, dtype=F32)
    diff = idx[:, None] - idx[None, :]
    din = jnp.where(diff >= 0, jnp.exp(log_g[:, None, None] * jnp.maximum(diff, 0.0)), 0.0)
    dq = jnp.exp(log_g[:, None] * (idx + 1.0))[..., None]
    dk = jnp.exp(log_g[:, None] * (c - 1.0 - idx))[..., None]
    return jnp.cos(ang), jnp.sin(ang), din, dq, dk


def ret_layer(x2, s, gain, w_in, q_gain, k_gain, gn_w, gn_b, w_out, *, tm):
    t, d = x2.shape
    c = RET_CHUNK
    dqk, dv = RET_QK_DIM, RET_V_DIM
    heads = w_out.shape[0] // dv
    tm = min(tm, s)
    tps = s // tm
    cos, sin, din, dq, dk = _ret_tables(s, heads)
    v_off = 2 * heads * dqk // dv
    g_off = v_off + heads
    return pl.pallas_call(
        functools.partial(_ret_layer_kernel, tiles_per_seq=tps),
        grid=(t // tm, heads),
        in_specs=[
            pl.BlockSpec((tm, d), lambda i, h: (i, 0)),
            pl.BlockSpec((1, d), lambda i, h: (0, 0)),
            pl.BlockSpec((d, dqk), lambda i, h: (0, h)),
            pl.BlockSpec((d, dqk), lambda i, h: (0, heads + h)),
            pl.BlockSpec((d, dv), lambda i, h: (0, v_off + h)),
            pl.BlockSpec((d, dv), lambda i, h: (0, g_off + h)),
            pl.BlockSpec((1, dqk), lambda i, h: (0, 0)),
            pl.BlockSpec((1, dqk), lambda i, h: (0, 0)),
            pl.BlockSpec((tm, dqk // 2), lambda i, h: (i % tps, 0)),
            pl.BlockSpec((tm, dqk // 2), lambda i, h: (i % tps, 0)),
            pl.BlockSpec((1, c, c), lambda i, h: (h, 0, 0)),
            pl.BlockSpec((1, c, 1), lambda i, h: (h, 0, 0)),
            pl.BlockSpec((1, c, 1), lambda i, h: (h, 0, 0)),
            pl.BlockSpec((1, dv), lambda i, h: (0, h)),
            pl.BlockSpec((1, dv), lambda i, h: (0, h)),
            pl.BlockSpec((dv, d), lambda i, h: (h, 0)),
        ],
        out_specs=pl.BlockSpec((tm, d), lambda i, h: (i, 0)),
        out_shape=jax.ShapeDtypeStruct((t, d), F32),
        scratch_shapes=[pltpu.VMEM((tm, d), BF16), pltpu.VMEM((heads, dqk, dv), F32)],
        compiler_params=_params("arbitrary", "arbitrary"),
        name="ret_layer",
    )(x2, gain.reshape(1, d), w_in, w_in, w_in, w_in, q_gain.reshape(1, dqk), k_gain.reshape(1, dqk),
      cos, sin, din, dq, dk, gn_w.reshape(1, heads * dv), gn_b.reshape(1, heads * dv), w_out)


def _sb_layer_kernel(x_ref, gain_ref, wq_ref, wk_ref, wv_ref, wg_ref, qg_ref, kg_ref, u_ref, wo_ref, o_ref,
                     h_ref, kc0_ref, kc1_ref, vc0_ref, vc1_ref, qs0_ref, qs1_ref, gs0_ref, gs1_ref,
                     acc_ref, run_ref, *, tb, hp, n_pairs, tiles_per_seq):
    i, p = pl.program_id(0), pl.program_id(1)
    d = SB_HEAD_DIM
    tm = x_ref.shape[0]
    nq = tm // tb
    row0 = pl.multiple_of((i % tiles_per_seq) * tm, tm)
    kc, vc, qs, gs = (kc0_ref, kc1_ref), (vc0_ref, vc1_ref), (qs0_ref, qs1_ref), (gs0_ref, gs1_ref)
    row = lax.broadcasted_iota(jnp.int32, (tb, tb), 0)
    col = lax.broadcasted_iota(jnp.int32, (tb, tb), 1)
    causal = col < row

    def rms(t, gain):
        return t * lax.rsqrt(jnp.mean(t * t, axis=-1, keepdims=True) + EPS) * gain

    def project_qk(par, slot):
        h = h_ref[...]
        q = jnp.dot(h, wq_ref[...], preferred_element_type=F32)
        k = jnp.dot(h, wk_ref[...], preferred_element_type=F32)
        for hh in range(hp):
            cs = slice(hh * d, (hh + 1) * d)
            qs[par][hh] = (rms(q[:, cs], qg_ref[...]) * (d ** -0.5)).astype(BF16)
            kc[par][slot * hp + hh, pl.ds(row0, tm), :] = rms(k[:, cs], kg_ref[...]).astype(BF16)

    def project_vg(par, slot):
        h = h_ref[...]
        v = jnp.dot(h, wv_ref[...], preferred_element_type=F32)
        gs[par][...] = jnp.dot(h, wg_ref[...], preferred_element_type=F32)
        for hh in range(hp):
            vc[par][slot * hp + hh, pl.ds(row0, tm), :] = v[:, hh * d:(hh + 1) * d].astype(BF16)

    def softplus(z):
        return jnp.maximum(z, 0.0) + jnp.log(1.0 + jnp.exp(-jnp.abs(z)))

    def suffix(sp):
        hi = sp.astype(BF16)
        lo = (sp - hi.astype(F32)).astype(BF16)
        return jnp.dot(jnp.concatenate([hi, lo], axis=1), u_ref[...], preferred_element_type=F32)

    def scores(qi, par, head, j0):
        kj = kc[par][head, pl.ds(j0, tb), :]
        return lax.dot_general(qi, kj, (((1,), (1,)), ((), ())), preferred_element_type=F32)

    def off_tile(qi, par, head, j0, acc, run):
        z = scores(qi, par, head, j0)
        sp = softplus(z)
        suf = suffix(sp)
        w = jnp.exp(z - sp - (suf + run))
        acc = acc + jnp.dot(w.astype(BF16), vc[par][head, pl.ds(j0, tb), :], preferred_element_type=F32)
        return acc, run + suf[:, 0:1] + sp[:, 0:1]

    def attend_first(par, slot, between):
        chains = [(qt, hh) for qt in range(nq) for hh in range(hp)]
        r0 = [pl.multiple_of(row0 + qt * tb, tb) for qt in range(nq)]
        j0 = [pl.multiple_of(jnp.maximum(r - tb, 0), tb) for r in r0]
        zd, zo, spd, spo, sufd, sufo = {}, {}, {}, {}, {}, {}
        for c in chains:
            qt, hh = c
            qi = qs[par][hh, qt * tb:(qt + 1) * tb, :]
            zd[c] = scores(qi, par, slot * hp + hh, r0[qt])
            zo[c] = scores(qi, par, slot * hp + hh, j0[qt])
        between[0]()
        for c in chains:
            spd[c] = jnp.where(causal, softplus(zd[c]), 0.0)
            spo[c] = softplus(zo[c])
            sufd[c] = suffix(spd[c])
            sufo[c] = suffix(spo[c])
        between[1]()
        accs, runs = {}, []
        for c in chains:
            qt, hh = c
            head = slot * hp + hh
            run = sufd[c][:, 0:1] + spd[c][:, 0:1]
            wd = jnp.where(causal, jnp.exp(zd[c] - spd[c] - sufd[c]), 0.0)
            acc = jnp.dot(wd.astype(BF16), vc[par][head, pl.ds(r0[qt], tb), :], preferred_element_type=F32)
            wo = jnp.exp(zo[c] - spo[c] - (sufo[c] + run))
            acc2 = acc + jnp.dot(wo.astype(BF16), vc[par][head, pl.ds(j0[qt], tb), :],
                                 preferred_element_type=F32)
            run2 = run + sufo[c][:, 0:1] + spo[c][:, 0:1]
            if qt == 0:
                acc2 = jnp.where(r0[qt] > 0, acc2, acc)
                run2 = jnp.where(r0[qt] > 0, run2, run)
            acc_ref[qt, hh] = acc2
            run_ref[qt, hh] = run2
            accs[c] = acc2
            runs.append(run2)
        return accs, jnp.min(functools.reduce(jnp.minimum, runs))

    def out_project(par, accs):
        a = jnp.concatenate(
            [jnp.concatenate([accs[(qt, hh)] for hh in range(hp)], axis=1) for qt in range(nq)], axis=0)
        y = (_silu(gs[par][...]) * a).astype(BF16)
        o_ref[...] += jnp.dot(y, wo_ref[...], preferred_element_type=F32)

    def attend_rest(par, slot, rmin_all):
        @pl.when(rmin_all <= SB_UNDERFLOW)
        def _():
            before = {(qt, hh): acc_ref[qt, hh] for qt in range(nq) for hh in range(hp)}
            for qt in range(nq):
                r0 = pl.multiple_of(row0 + qt * tb, tb)
                for hh in range(hp):
                    def more(c):
                        j, r = c
                        return jnp.logical_and(j >= 0, r <= SB_UNDERFLOW)

                    def ktile(c, qt=qt, hh=hh):
                        j, _ = c
                        qi = qs[par][hh, qt * tb:(qt + 1) * tb, :]
                        a, r = off_tile(qi, par, slot * hp + hh, pl.multiple_of(j * tb, tb),
                                        acc_ref[qt, hh], run_ref[qt, hh])
                        acc_ref[qt, hh] = a
                        run_ref[qt, hh] = r
                        return j - 1, jnp.min(r)

                    lax.while_loop(more, ktile, (r0 // tb - 2, jnp.min(run_ref[qt, hh])))
            out_project(par, {c: acc_ref[c[0], c[1]] - before[c] for c in before})

    def step(attend_par, project_par):
        if project_par is None:
            between = (lambda: None, lambda: None)
        else:
            between = (functools.partial(project_qk, project_par, p // 2),
                       functools.partial(project_vg, project_par, p // 2))
        if attend_par is None:
            between[0]()
            between[1]()
        else:
            a_slot = (p - 1) // 2
            accs, rmin_all = attend_first(attend_par, a_slot, between)
            out_project(attend_par, accs)
            attend_rest(attend_par, a_slot, rmin_all)

    last_par = (n_pairs - 1) % 2

    @pl.when(p == 0)
    def _():
        _rmsnorm_rows(x_ref, gain_ref, h_ref, 128, copy_ref=o_ref)
        step(None, 0)

    @pl.when(jnp.logical_and(jnp.logical_and(p > 0, p < n_pairs), p % 2 == 1))
    def _():
        step(0, 1)

    @pl.when(jnp.logical_and(jnp.logical_and(p > 0, p < n_pairs), p % 2 == 0))
    def _():
        step(1, 0)

    @pl.when(p == n_pairs)
    def _():
        step(last_par, None)


def sb_layer(x2, s, gain, w_in, q_gain, k_gain, w_out, *, tm, tb, hp):
    t, dm = x2.shape
    d = SB_HEAD_DIM
    width = w_out.shape[0]
    heads = width // d
    tm, tb = min(tm, s), min(tb, s)
    n_pairs = heads // hp
    half = (n_pairs + 1) // 2
    u = (jnp.arange(tb)[:, None] > jnp.arange(tb)[None, :]).astype(BF16)
    u = jnp.concatenate([u, u], axis=0)

    def stream(g):
        return pl.BlockSpec((dm, hp * d), lambda i, p: (0, g * n_pairs + jnp.minimum(p, n_pairs - 1)))

    cache = pltpu.VMEM((half * hp, s, d), BF16)
    return pl.pallas_call(
        functools.partial(_sb_layer_kernel, tb=tb, hp=hp, n_pairs=n_pairs, tiles_per_seq=s // tm),
        grid=(t // tm, n_pairs + 1),
        in_specs=[pl.BlockSpec((tm, dm), lambda i, p: (i, 0)),
                  pl.BlockSpec((1, dm), lambda i, p: (0, 0)),
                  stream(0), stream(1), stream(2), stream(3),
                  pl.BlockSpec((1, d), lambda i, p: (0, 0)),
                  pl.BlockSpec((1, d), lambda i, p: (0, 0)),
                  pl.BlockSpec((2 * tb, tb), lambda i, p: (0, 0)),
                  pl.BlockSpec((hp * d, dm), lambda i, p: (jnp.maximum(p - 1, 0), 0))],
        out_specs=pl.BlockSpec((tm, dm), lambda i, p: (i, 0)),
        out_shape=jax.ShapeDtypeStruct((t, dm), F32),
        scratch_shapes=[pltpu.VMEM((tm, dm), BF16), cache, cache, cache, cache,
                        pltpu.VMEM((hp, tm, d), BF16), pltpu.VMEM((hp, tm, d), BF16),
                        pltpu.VMEM((tm, hp * d), F32), pltpu.VMEM((tm, hp * d), F32),
                        pltpu.VMEM((tm // tb, hp, tb, d), F32), pltpu.VMEM((tm // tb, hp, tb, 1), F32)],
        compiler_params=_params("arbitrary", "arbitrary"),
        name="sb_layer",
    )(x2, gain.reshape(1, dm), w_in, w_in, w_in, w_in, q_gain.reshape(1, d), k_gain.reshape(1, d), u, w_out)


def kernel(x, conv_norm, conv_w_in, conv_w, conv_b, conv_w_out, ret_norm, ret_w_in, ret_q_gain, ret_k_gain,
           ret_gn_w, ret_gn_b, ret_w_out, sb_norm, sb_w_in, sb_q_gain, sb_k_gain, sb_w_out):
    bsz, s, d = x.shape
    depth = conv_norm.shape[0] + ret_norm.shape[0] + sb_norm.shape[0]
    x2 = x.reshape(bsz * s, d)
    for i in range(depth):
        kind, j = i % N_MIXERS, i // N_MIXERS
        if kind == 0:
            x2 = conv_layer(x2, s, conv_norm[j], conv_w_in[j].astype(BF16), conv_w[j], conv_b[j],
                            conv_w_out[j].astype(BF16), tm=512, ec=512)
        elif kind == 1:
            x2 = ret_layer(x2, s, ret_norm[j], ret_w_in[j].astype(BF16), ret_q_gain[j], ret_k_gain[j],
                           ret_gn_w[j], ret_gn_b[j], ret_w_out[j].astype(BF16), tm=512)
        else:
            x2 = sb_layer(x2, s, sb_norm[j], sb_w_in[j].astype(BF16), sb_q_gain[j], sb_k_gain[j],
                          sb_w_out[j].astype(BF16), tm=512, tb=256, hp=2)
    return x2.reshape(bsz, s, d)
```

```python
import functools

import jax
import jax.numpy as jnp
from jax import lax
from jax.experimental import pallas as pl
from jax.experimental.pallas import tpu as pltpu

EPS = 1e-6
GN_EPS = 1e-5
N_MIXERS = 3
CONV_K = 3
RET_QK_DIM = 256
RET_V_DIM = 512
RET_CHUNK = 128
ROPE_BASE = 10000.0
SB_HEAD_DIM = 128
SB_UNDERFLOW = 104.0

F32 = jnp.float32
BF16 = jnp.bfloat16

VMEM_LIMIT_BYTES = 56 * 1024 * 1024


def _params(*sem):
    return pltpu.CompilerParams(dimension_semantics=sem, vmem_limit_bytes=VMEM_LIMIT_BYTES)


def _silu(g):
    return g * jax.nn.sigmoid(g)


BF16_SUBLANES = 16


def _layer_call(kernel_fn, *, grid, in_specs, out_spec, out_shape, scratch_shapes, name, inputs,
                side=(), side_blocks=1, side_index=None):
    n_in, n_side = len(inputs), len(side)

    def body(*refs):
        side_in, o_ref = refs[n_in:n_in + n_side], refs[n_in + n_side]
        side_out = refs[n_in + n_side + 1:n_in + 2 * n_side + 1]
        for ci, co in zip(side_in, side_out):
            co[...] = ci[0].astype(co.dtype)
        kernel_fn(*refs[:n_in], o_ref, *refs[n_in + 2 * n_side + 1:])

    side_in_specs, side_out_specs, side_shapes = [], [], []
    for w, layer in side:
        _, rows, cols = w.shape
        nblk = min(side_blocks, rows // BF16_SUBLANES)
        assert rows % nblk == 0 and side_blocks % nblk == 0
        rep = side_blocks // nblk
        side_in_specs.append(pl.BlockSpec(
            (1, rows // nblk, cols), lambda i, j, layer=layer, rep=rep: (layer, side_index(i, j) // rep, 0)))
        side_out_specs.append(pl.BlockSpec(
            (rows // nblk, cols), lambda i, j, rep=rep: (side_index(i, j) // rep, 0)))
        side_shapes.append(jax.ShapeDtypeStruct((rows, cols), BF16))
    outs = pl.pallas_call(
        body,
        grid=grid,
        in_specs=list(in_specs) + side_in_specs,
        out_specs=[out_spec] + side_out_specs,
        out_shape=[out_shape] + side_shapes,
        scratch_shapes=scratch_shapes,
        compiler_params=_params("arbitrary", "arbitrary"),
        name=name,
    )(*inputs, *[w for w, _ in side])
    return outs[0], list(outs[1:])


def _rmsnorm_rows(x_ref, g_ref, h_ref, rows, copy_ref=None):
    def body(r, c):
        sl = pl.ds(pl.multiple_of(r * rows, rows), rows)
        x = x_ref[sl, :]
        ms = jnp.mean(x * x, axis=-1, keepdims=True)
        h_ref[sl, :] = (x * lax.rsqrt(ms + EPS) * g_ref[...]).astype(h_ref.dtype)
        if copy_ref is not None:
            copy_ref[sl, :] = x
        return c

    lax.fori_loop(0, x_ref.shape[0] // rows, body, 0)


def _conv_layer_kernel(x_ref, xn_ref, gain_ref, wb_ref, wc_ref, wu_ref, wg_ref, cw_ref, cb_ref, wo_ref, o_ref,
                       h0_ref, h1_ref, carry_ref, *, tiles_per_seq):
    i, j = pl.program_id(0), pl.program_id(1)
    tm = x_ref.shape[0]
    rows = tm // pl.num_programs(1)

    @pl.when(jnp.logical_and(i == 0, j == 0))
    def _():
        _rmsnorm_rows(x_ref, gain_ref, h0_ref, 128)

    def step(h_ref, hn_ref, first):
        h = h_ref[...]
        v = (jnp.dot(h, wc_ref[...], preferred_element_type=F32)
             * jnp.dot(h, wu_ref[...], preferred_element_type=F32))
        gate = _silu(jnp.dot(h, wg_ref[...], preferred_element_type=F32))
        prev = jnp.where(i % tiles_per_seq == 0, 0.0, carry_ref[j])
        carry_ref[j] = v[tm - 8:, :]
        p1, p2 = prev[7:8, :], prev[6:7, :]
        row = lax.broadcasted_iota(jnp.int32, v.shape, 0)
        v1 = jnp.where(row >= 1, pltpu.roll(v, 1, 0), p1)
        v2 = jnp.where(row >= 2, pltpu.roll(v, 2, 0), jnp.where(row == 1, p1, p2))
        conv = cb_ref[...] + v2 * cw_ref[0:1, :]
        conv = conv + v1 * cw_ref[1:2, :]
        conv = conv + v * cw_ref[2:3, :]
        y = jnp.dot(h, wb_ref[...], preferred_element_type=F32) * conv
        y = y * gate
        contrib = jnp.dot(y.astype(BF16), wo_ref[...], preferred_element_type=F32)
        if first:
            o_ref[...] = x_ref[...] + contrib
        else:
            o_ref[...] += contrib
        sl = pl.ds(pl.multiple_of(j * rows, rows), rows)
        xn = xn_ref[sl, :]
        ms = jnp.mean(xn * xn, axis=-1, keepdims=True)
        hn_ref[sl, :] = (xn * lax.rsqrt(ms + EPS) * gain_ref[...]).astype(hn_ref.dtype)

    for parity, (h_ref, hn_ref) in enumerate(((h0_ref, h1_ref), (h1_ref, h0_ref))):
        for first in (True, False):
            pl.when(jnp.logical_and(i % 2 == parity, (j == 0) == first))(
                functools.partial(step, h_ref, hn_ref, first))


def conv_layer(x2, s, gain, w_in, conv_w, conv_b, w_out, *, tm, ec, side=()):
    t, d = x2.shape
    e = w_out.shape[0]
    tm, ec = min(tm, s), min(ec, e)
    nj = e // ec
    ni = t // tm
    assert tm % nj == 0 and (tm // nj) % 8 == 0

    def stream(g):
        return pl.BlockSpec((d, ec), lambda i, j: (0, g * nj + j))

    return _layer_call(
        functools.partial(_conv_layer_kernel, tiles_per_seq=s // tm),
        grid=(ni, nj),
        in_specs=[pl.BlockSpec((tm, d), lambda i, j: (i, 0)),
                  pl.BlockSpec((tm, d), lambda i, j: (jnp.minimum(i + 1, ni - 1), 0)),
                  pl.BlockSpec((1, d), lambda i, j: (0, 0)),
                  stream(0), stream(1), stream(2), stream(3),
                  pl.BlockSpec((CONV_K, ec), lambda i, j: (0, j)),
                  pl.BlockSpec((1, ec), lambda i, j: (0, j)),
                  pl.BlockSpec((ec, d), lambda i, j: (j, 0))],
        out_spec=pl.BlockSpec((tm, d), lambda i, j: (i, 0)),
        out_shape=jax.ShapeDtypeStruct((t, d), F32),
        scratch_shapes=[pltpu.VMEM((tm, d), BF16), pltpu.VMEM((tm, d), BF16), pltpu.VMEM((nj, 8, ec), F32)],
        name="conv_layer",
        inputs=(x2, x2, gain.reshape(1, d), w_in, w_in, w_in, w_in, conv_w, conv_b.reshape(1, e), w_out),
        side=side, side_blocks=ni * nj, side_index=lambda i, j: i * nj + j)


def _ret_layer_kernel(x_ref, gain_ref, wq_ref, wk_ref, wv_ref, wg_ref, qg_ref, kg_ref, cos_ref, sin_ref,
                      din_ref, dq_ref, dk_ref, gw_ref, gb_ref, wo_ref, o_ref, h_ref, state_ref, *, tiles_per_seq):
    i, hd = pl.program_id(0), pl.program_id(1)
    c = RET_CHUNK
    half = RET_QK_DIM // 2

    @pl.when(hd == 0)
    def _():
        _rmsnorm_rows(x_ref, gain_ref, h_ref, 128, copy_ref=o_ref)

    @pl.when(i % tiles_per_seq == 0)
    def _():
        state_ref[hd] = jnp.zeros(state_ref.shape[1:], F32)

    h = h_ref[...]
    cos, sin = cos_ref[...], sin_ref[...]

    def norm_rot(w_ref, gain):
        t = jnp.dot(h, w_ref[...], preferred_element_type=F32)
        t = t * lax.rsqrt(jnp.mean(t * t, axis=-1, keepdims=True) + EPS) * gain
        t1, t2 = t[:, :half], t[:, half:]
        return jnp.concatenate([t1 * cos - t2 * sin, t2 * cos + t1 * sin], axis=-1)

    q = norm_rot(wq_ref, qg_ref[...]).astype(BF16)
    k = norm_rot(wk_ref, kg_ref[...]) * (RET_QK_DIM ** -0.5)
    v = jnp.dot(h, wv_ref[...], preferred_element_type=F32).astype(BF16)

    din = din_ref[0]
    dq = dq_ref[0]
    dk = dk_ref[0]
    dchunk = dq[c - 1:c, :]
    chunks = [slice(ci * c, (ci + 1) * c) for ci in range(q.shape[0] // c)]

    inner = [lax.dot_general(q[sl], k[sl].astype(BF16), (((1,), (1,)), ((), ())),
                             preferred_element_type=F32) * din for sl in chunks]
    kv = [lax.dot_general((k[sl] * dk).astype(BF16), v[sl], (((0,), (0,)), ((), ())),
                          preferred_element_type=F32) for sl in chunks]
    states = [state_ref[hd]]
    for ci in range(len(chunks)):
        states.append(states[ci] * dchunk + kv[ci])
    state_ref[hd] = states[-1]
    outs = []
    for ci, sl in enumerate(chunks):
        o = jnp.dot(inner[ci].astype(BF16), v[sl], preferred_element_type=F32)
        outs.append(o + jnp.dot(q[sl], states[ci].astype(BF16), preferred_element_type=F32) * dq)
    g = jnp.dot(h, wg_ref[...], preferred_element_type=F32)

    ys = []
    for ci, sl in enumerate(chunks):
        o = outs[ci]
        mu = jnp.mean(o, axis=-1, keepdims=True)
        var = jnp.mean(jnp.square(o - mu), axis=-1, keepdims=True)
        on = (o - mu) * lax.rsqrt(var + GN_EPS)
        on = on * gw_ref[...] + gb_ref[...]
        ys.append((_silu(g[sl]) * on).astype(BF16))
    y = jnp.concatenate(ys, axis=0)
    o_ref[...] += jnp.dot(y, wo_ref[...], preferred_element_type=F32)


def _ret_tables(s, heads):
    c = RET_CHUNK
    half = RET_QK_DIM // 2
    inv_freq = ROPE_BASE ** (-jnp.arange(half, dtype=F32) / half)
    ang = jnp.arange(s).astype(F32)[:, None] * inv_freq[None, :]
    log_g = jnp.log(1.0 - jnp.exp2(-5.0 - jnp.arange(heads, dtype=F32)))
    idx = jnp.arange(c, dtype=F32)
    diff = idx[:, None] - idx[None, :]
    din = jnp.where(diff >= 0, jnp.exp(log_g[:, None, None] * jnp.maximum(diff, 0.0)), 0.0)
    dq = jnp.exp(log_g[:, None] * (idx + 1.0))[..., None]
    dk = jnp.exp(log_g[:, None] * (c - 1.0 - idx))[..., None]
    return jnp.cos(ang), jnp.sin(ang), din, dq, dk


def ret_layer(x2, s, gain, w_in, q_gain, k_gain, gn_w, gn_b, w_out, *, tm, side=()):
    t, d = x2.shape
    c = RET_CHUNK
    dqk, dv = RET_QK_DIM, RET_V_DIM
    heads = w_out.shape[0] // dv
    tm = min(tm, s)
    tps = s // tm
    cos, sin, din, dq, dk = _ret_tables(s, heads)
    v_off = 2 * heads * dqk // dv
    g_off = v_off + heads
    return _layer_call(
        functools.partial(_ret_layer_kernel, tiles_per_seq=tps),
        grid=(t // tm, heads),
        in_specs=[
            pl.BlockSpec((tm, d), lambda i, h: (i, 0)),
            pl.BlockSpec((1, d), lambda i, h: (0, 0)),
            pl.BlockSpec((d, dqk), lambda i, h: (0, h)),
            pl.BlockSpec((d, dqk), lambda i, h: (0, heads + h)),
            pl.BlockSpec((d, dv), lambda i, h: (0, v_off + h)),
            pl.BlockSpec((d, dv), lambda i, h: (0, g_off + h)),
            pl.BlockSpec((1, dqk), lambda i, h: (0, 0)),
            pl.BlockSpec((1, dqk), lambda i, h: (0, 0)),
            pl.BlockSpec((tm, dqk // 2), lambda i, h: (i % tps, 0)),
            pl.BlockSpec((tm, dqk // 2), lambda i, h: (i % tps, 0)),
            pl.BlockSpec((1, c, c), lambda i, h: (h, 0, 0)),
            pl.BlockSpec((1, c, 1), lambda i, h: (h, 0, 0)),
            pl.BlockSpec((1, c, 1), lambda i, h: (h, 0, 0)),
            pl.BlockSpec((1, dv), lambda i, h: (0, h)),
            pl.BlockSpec((1, dv), lambda i, h: (0, h)),
            pl.BlockSpec((dv, d), lambda i, h: (h, 0)),
        ],
        out_spec=pl.BlockSpec((tm, d), lambda i, h: (i, 0)),
        out_shape=jax.ShapeDtypeStruct((t, d), F32),
        scratch_shapes=[pltpu.VMEM((tm, d), BF16), pltpu.VMEM((heads, dqk, dv), F32)],
        name="ret_layer",
        inputs=(x2, gain.reshape(1, d), w_in, w_in, w_in, w_in, q_gain.reshape(1, dqk), k_gain.reshape(1, dqk),
                cos, sin, din, dq, dk, gn_w.reshape(1, heads * dv), gn_b.reshape(1, heads * dv), w_out),
        side=side, side_blocks=(t // tm) * heads, side_index=lambda i, h: i * heads + h)


def _sb_layer_kernel(x_ref, gain_ref, wq_ref, wk_ref, wv_ref, wg_ref, qg_ref, kg_ref, u_ref, wo_ref, o_ref,
                     h_ref, kc0_ref, kc1_ref, vc0_ref, vc1_ref, qs0_ref, qs1_ref, gs0_ref, gs1_ref,
                     acc_ref, run_ref, *, tb, hp, n_pairs, tiles_per_seq):
    i, p = pl.program_id(0), pl.program_id(1)
    d = SB_HEAD_DIM
    tm = x_ref.shape[0]
    nq = tm // tb
    row0 = pl.multiple_of((i % tiles_per_seq) * tm, tm)
    kc, vc, qs, gs = (kc0_ref, kc1_ref), (vc0_ref, vc1_ref), (qs0_ref, qs1_ref), (gs0_ref, gs1_ref)
    row = lax.broadcasted_iota(jnp.int32, (tb, tb), 0)
    col = lax.broadcasted_iota(jnp.int32, (tb, tb), 1)
    causal = col < row

    def rms(t, gain):
        return t * lax.rsqrt(jnp.mean(t * t, axis=-1, keepdims=True) + EPS) * gain

    def project_qk(par, slot):
        h = h_ref[...]
        q = jnp.dot(h, wq_ref[...], preferred_element_type=F32)
        k = jnp.dot(h, wk_ref[...], preferred_element_type=F32)
        for hh in range(hp):
            cs = slice(hh * d, (hh + 1) * d)
            qs[par][hh] = (rms(q[:, cs], qg_ref[...]) * (d ** -0.5)).astype(BF16)
            kc[par][slot * hp + hh, pl.ds(row0, tm), :] = rms(k[:, cs], kg_ref[...]).astype(BF16)

    def project_vg(par, slot):
        h = h_ref[...]
        v = jnp.dot(h, wv_ref[...], preferred_element_type=F32)
        gs[par][...] = jnp.dot(h, wg_ref[...], preferred_element_type=F32)
        for hh in range(hp):
            vc[par][slot * hp + hh, pl.ds(row0, tm), :] = v[:, hh * d:(hh + 1) * d].astype(BF16)

    def softplus(z):
        return jnp.maximum(z, 0.0) + jnp.log(1.0 + jnp.exp(-jnp.abs(z)))

    def suffix(sp):
        hi = sp.astype(BF16)
        lo = (sp - hi.astype(F32)).astype(BF16)
        return jnp.dot(jnp.concatenate([hi, lo], axis=1), u_ref[...], preferred_element_type=F32)

    def scores(qi, par, head, j0):
        kj = kc[par][head, pl.ds(j0, tb), :]
        return lax.dot_general(qi, kj, (((1,), (1,)), ((), ())), preferred_element_type=F32)

    def off_tile(qi, par, head, j0, acc, run):
        z = scores(qi, par, head, j0)
        sp = softplus(z)
        suf = suffix(sp)
        w = jnp.exp(z - sp - (suf + run))
        acc = acc + jnp.dot(w.astype(BF16), vc[par][head, pl.ds(j0, tb), :], preferred_element_type=F32)
        return acc, run + suf[:, 0:1] + sp[:, 0:1]

    def attend_first(par, slot, between):
        chains = [(qt, hh) for qt in range(nq) for hh in range(hp)]
        r0 = [pl.multiple_of(row0 + qt * tb, tb) for qt in range(nq)]
        j0 = [pl.multiple_of(jnp.maximum(r - tb, 0), tb) for r in r0]
        zd, zo, spd, spo, sufd, sufo = {}, {}, {}, {}, {}, {}
        for c in chains:
            qt, hh = c
            qi = qs[par][hh, qt * tb:(qt + 1) * tb, :]
            zd[c] = scores(qi, par, slot * hp + hh, r0[qt])
            zo[c] = scores(qi, par, slot * hp + hh, j0[qt])
        between[0]()
        for c in chains:
            spd[c] = jnp.where(causal, softplus(zd[c]), 0.0)
            spo[c] = softplus(zo[c])
            sufd[c] = suffix(spd[c])
            sufo[c] = suffix(spo[c])
        between[1]()
        rmin = []
        for c in chains:
            qt, hh = c
            head = slot * hp + hh
            run = sufd[c][:, 0:1] + spd[c][:, 0:1]
            wd = jnp.where(causal, jnp.exp(zd[c] - spd[c] - sufd[c]), 0.0)
            acc = jnp.dot(wd.astype(BF16), vc[par][head, pl.ds(r0[qt], tb), :], preferred_element_type=F32)
            wo = jnp.exp(zo[c] - spo[c] - (sufo[c] + run))
            acc2 = acc + jnp.dot(wo.astype(BF16), vc[par][head, pl.ds(j0[qt], tb), :],
                                 preferred_element_type=F32)
            run2 = run + sufo[c][:, 0:1] + spo[c][:, 0:1]
            if qt == 0:
                acc2 = jnp.where(r0[qt] > 0, acc2, acc)
                run2 = jnp.where(r0[qt] > 0, run2, run)
            acc_ref[qt, hh] = acc2
            run_ref[qt, hh] = run2
            rmin.append(jnp.min(run2))
        return rmin

    def attend_rest(par, slot, rmin):
        for qt in range(nq):
            r0 = pl.multiple_of(row0 + qt * tb, tb)
            for hh in range(hp):
                def more(c):
                    j, r = c
                    return jnp.logical_and(j >= 0, r <= SB_UNDERFLOW)

                def ktile(c, qt=qt, hh=hh):
                    j, _ = c
                    qi = qs[par][hh, qt * tb:(qt + 1) * tb, :]
                    a, r = off_tile(qi, par, slot * hp + hh, pl.multiple_of(j * tb, tb),
                                    acc_ref[qt, hh], run_ref[qt, hh])
                    acc_ref[qt, hh] = a
                    run_ref[qt, hh] = r
                    return j - 1, jnp.min(r)

                lax.while_loop(more, ktile, (r0 // tb - 2, rmin[qt * hp + hh]))

    def finish(par):
        g = gs[par][...]
        y = jnp.concatenate(
            [jnp.concatenate([acc_ref[qt, hh] for hh in range(hp)], axis=1) for qt in range(nq)], axis=0)
        y = (_silu(g) * y).astype(BF16)
        o_ref[...] += jnp.dot(y, wo_ref[...], preferred_element_type=F32)

    def step(attend_par, project_par):
        if project_par is None:
            between = (lambda: None, lambda: None)
        else:
            between = (functools.partial(project_qk, project_par, p // 2),
                       functools.partial(project_vg, project_par, p // 2))
        if attend_par is None:
            between[0]()
            between[1]()
        else:
            a_slot = (p - 1) // 2
            rmin = attend_first(attend_par, a_slot, between)
            attend_rest(attend_par, a_slot, rmin)
            finish(attend_par)

    last_par = (n_pairs - 1) % 2

    @pl.when(p == 0)
    def _():
        _rmsnorm_rows(x_ref, gain_ref, h_ref, 128, copy_ref=o_ref)
        step(None, 0)

    @pl.when(jnp.logical_and(jnp.logical_and(p > 0, p < n_pairs), p % 2 == 1))
    def _():
        step(0, 1)

    @pl.when(jnp.logical_and(jnp.logical_and(p > 0, p < n_pairs), p % 2 == 0))
    def _():
        step(1, 0)

    @pl.when(p == n_pairs)
    def _():
        step(last_par, None)


def sb_layer(x2, s, gain, w_in, q_gain, k_gain, w_out, *, tm, tb, hp, side=()):
    t, dm = x2.shape
    d = SB_HEAD_DIM
    width = w_out.shape[0]
    heads = width // d
    tm, tb = min(tm, s), min(tb, s)
    n_pairs = heads // hp
    half = (n_pairs + 1) // 2
    u = (jnp.arange(tb)[:, None] > jnp.arange(tb)[None, :]).astype(BF16)
    u = jnp.concatenate([u, u], axis=0)

    def stream(g):
        return pl.BlockSpec((dm, hp * d), lambda i, p: (0, g * n_pairs + jnp.minimum(p, n_pairs - 1)))

    cache = pltpu.VMEM((half * hp, s, d), BF16)
    return _layer_call(
        functools.partial(_sb_layer_kernel, tb=tb, hp=hp, n_pairs=n_pairs, tiles_per_seq=s // tm),
        grid=(t // tm, n_pairs + 1),
        in_specs=[pl.BlockSpec((tm, dm), lambda i, p: (i, 0)),
                  pl.BlockSpec((1, dm), lambda i, p: (0, 0)),
                  stream(0), stream(1), stream(2), stream(3),
                  pl.BlockSpec((1, d), lambda i, p: (0, 0)),
                  pl.BlockSpec((1, d), lambda i, p: (0, 0)),
                  pl.BlockSpec((2 * tb, tb), lambda i, p: (0, 0)),
                  pl.BlockSpec((hp * d, dm), lambda i, p: (jnp.maximum(p - 1, 0), 0))],
        out_spec=pl.BlockSpec((tm, dm), lambda i, p: (i, 0)),
        out_shape=jax.ShapeDtypeStruct((t, dm), F32),
        scratch_shapes=[pltpu.VMEM((tm, dm), BF16), cache, cache, cache, cache,
                        pltpu.VMEM((hp, tm, d), BF16), pltpu.VMEM((hp, tm, d), BF16),
                        pltpu.VMEM((tm, hp * d), F32), pltpu.VMEM((tm, hp * d), F32),
                        pltpu.VMEM((tm // tb, hp, tb, d), F32), pltpu.VMEM((tm // tb, hp, tb, 1), F32)],
        name="sb_layer",
        inputs=(x2, gain.reshape(1, dm), w_in, w_in, w_in, w_in, q_gain.reshape(1, d), k_gain.reshape(1, d),
                u, w_out),
        side=side, side_blocks=(t // tm) * n_pairs,
        side_index=lambda i, p: i * n_pairs + jnp.minimum(p, n_pairs - 1))


def kernel(x, conv_norm, conv_w_in, conv_w, conv_b, conv_w_out, ret_norm, ret_w_in, ret_q_gain, ret_k_gain,
           ret_gn_w, ret_gn_b, ret_w_out, sb_norm, sb_w_in, sb_q_gain, sb_k_gain, sb_w_out):
    bsz, s, d = x.shape
    depth = conv_norm.shape[0] + ret_norm.shape[0] + sb_norm.shape[0]
    x2 = x.reshape(bsz * s, d)
    w_ins, w_outs = (conv_w_in, ret_w_in, sb_w_in), (conv_w_out, ret_w_out, sb_w_out)

    def stacked_weights(i):
        return [(w_ins[i % N_MIXERS], i // N_MIXERS), (w_outs[i % N_MIXERS], i // N_MIXERS)]

    w_in, w_out = (w[j].astype(BF16) for w, j in stacked_weights(0))
    for i in range(depth):
        kind, j = i % N_MIXERS, i // N_MIXERS
        side = stacked_weights(i + 1) if i + 1 < depth else ()
        if kind == 0:
            x2, cast = conv_layer(x2, s, conv_norm[j], w_in, conv_w[j], conv_b[j], w_out, tm=512, ec=512, side=side)
        elif kind == 1:
            x2, cast = ret_layer(x2, s, ret_norm[j], w_in, ret_q_gain[j], ret_k_gain[j], ret_gn_w[j], ret_gn_b[j],
                                 w_out, tm=512, side=side)
        else:
            x2, cast = sb_layer(x2, s, sb_norm[j], w_in, sb_q_gain[j], sb_k_gain[j], w_out,
                                tm=512, tb=256, hp=2, side=side)
        if cast:
            w_in, w_out = cast
    return x2.reshape(bsz, s, d)
```
